```python
import math
import jax, jax.numpy as jnp
from jax import lax
import numpy as np

D_MODEL = 2048
BATCH = 4
SEQ = 2048
DEPTH = 4
DEC_BATCH = 8
DEC_SEQ = 1
PAST_LEN = 16384
PAGE_SIZE = 128

H_DN = 8
DN_DK = 128
DN_DV = 128
CONV_W = 4
DN_CHUNK = 64
CONV_DIM = H_DN * (2 * DN_DK + DN_DV)
H_DIFF = 8
DIFF_DQ = 64
DIFF_DV = 128
QBLOCK = 128
NUM_BUCKETS = 32
MAX_DISTANCE = 128
D_FF = 4 * D_MODEL
EPS = 1e-6

SPLIT_SIZES = (H_DN * DN_DK, H_DN * DN_DK, H_DN * DN_DV, H_DN, H_DN, H_DN * DN_DV,
               H_DIFF * 2 * DIFF_DQ, H_DIFF * 2 * DIFF_DQ, H_DIFF * DIFF_DV, 2 * D_MODEL)
IN_COLS = sum(SPLIT_SIZES)
SPLIT_POINTS = tuple(sum(SPLIT_SIZES[:i + 1]) for i in range(len(SPLIT_SIZES) - 1))

kernel_name = "hybrid_gdn_diffattn_decode_step"

F32 = jnp.float32


def rms_norm(x, gain):
    xf = x.astype(F32)
    y = xf * lax.rsqrt(jnp.mean(xf * xf, axis=-1, keepdims=True) + EPS)
    return (y * gain.astype(F32)).astype(x.dtype)


def l2_normalize(x):
    xf = x.astype(F32)
    return xf * lax.rsqrt(jnp.sum(xf * xf, axis=-1, keepdims=True) + EPS)


def t5_bucket(rel):
    max_exact = NUM_BUCKETS // 2
    n = jnp.maximum(rel, 0)
    nf = jnp.maximum(n, max_exact).astype(F32)
    large = max_exact + (jnp.log(nf / max_exact) / math.log(MAX_DISTANCE / max_exact)
                         * (NUM_BUCKETS - max_exact)).astype(jnp.int32)
    large = jnp.minimum(large, NUM_BUCKETS - 1)
    return jnp.where(n < max_exact, n, large)


def gated_delta_chunked(q, k, v, beta, g, s0):
    B, T, H, DK = q.shape
    C = min(DN_CHUNK, T)
    n = -(-T // C)
    pad = n * C - T

    def to_chunks(a):
        a = jnp.pad(a, [(0, 0), (0, pad)] + [(0, 0)] * (a.ndim - 2))
        a = a.reshape((B, n, C) + a.shape[2:])
        return jnp.moveaxis(a, (1, 3), (0, 2))

    q, k, v, beta, g = [to_chunks(a) for a in (q, k, v, beta, g)]
    g = jnp.cumsum(g, axis=-1)
    idx = jnp.arange(C)
    tril = idx[:, None] >= idx[None, :]
    strict = idx[:, None] > idx[None, :]
    decay = jnp.exp(jnp.where(tril, g[..., :, None] - g[..., None, :], -jnp.inf))
    kb = k * beta[..., None]
    vb = v * beta[..., None]
    lower = jnp.where(strict, jnp.einsum('nbhid,nbhjd->nbhij', kb, k) * decay, 0.0)
    eye = jnp.eye(C, dtype=F32)
    tmat = lax.linalg.triangular_solve(eye + lower, jnp.broadcast_to(eye, lower.shape),
                                       left_side=True, lower=True, unit_diagonal=True)
    value = tmat @ vb
    kcum = tmat @ (kb * jnp.exp(g)[..., None])
    qk = jnp.einsum('nbhid,nbhjd->nbhij', q, k) * decay
    g_last = g[..., -1]

    def step(S, xs):
        q_c, k_c, val_c, kcum_c, qk_c, g_c, gl_c = xs
        v_new = val_c - jnp.einsum('bhcd,bhde->bhce', kcum_c, S)
        o = (jnp.einsum('bhcd,bhde->bhce', q_c * jnp.exp(g_c)[..., None], S)
             + jnp.einsum('bhij,bhje->bhie', qk_c, v_new))
        S = (S * jnp.exp(gl_c)[..., None, None]
             + jnp.einsum('bhcd,bhce->bhde', k_c * jnp.exp(gl_c[..., None] - g_c)[..., None], v_new))
        return S, o

    s_fin, o = lax.scan(step, s0, (q, k, value, kcum, qk, g, g_last))
    o = jnp.moveaxis(o, (0, 2), (1, 3))
    o = o.reshape(B, n * C, H, v.shape[-1])[:, :T]
    return o, s_fin


def diff_attention(q, k, v, q_start, lam, rel_bias):
    B, Tq = q.shape[:2]
    Tk = k.shape[1]
    blk = min(QBLOCK, Tq)
    nb = -(-Tq // blk)
    pad = nb * blk - Tq
    qb = jnp.pad(q, ((0, 0), (0, pad), (0, 0), (0, 0), (0, 0)))
    qb = qb.reshape(B, nb, blk, H_DIFF, 2, DIFF_DQ).swapaxes(0, 1)
    q_pos = (q_start + jnp.arange(nb * blk, dtype=jnp.int32)).reshape(nb, blk)
    k_pos = jnp.arange(Tk, dtype=jnp.int32)
    kf = k.astype(F32)
    vf = v.astype(F32)
    bias_tab = rel_bias.astype(F32)
    scale = DIFF_DQ ** -0.5

    def one_block(args):
        qblk, qp = args
        rel = qp[:, None] - k_pos[None, :]
        bias = jnp.transpose(bias_tab[t5_bucket(rel)], (2, 0, 1))
        s = jnp.einsum('bqhmd,bkhmd->bmhqk', qblk.astype(F32), kf) * scale + bias
        s = jnp.where(rel >= 0, s, -jnp.inf)
        p = jax.nn.softmax(s, axis=-1)
        w = p[:, 0] - lam * p[:, 1]
        return jnp.einsum('bhqk,bkhd->bqhd', w, vf)

    o = lax.map(one_block, (qb, q_pos))
    return o.swapaxes(0, 1).reshape(B, nb * blk, H_DIFF, DIFF_DV)[:, :Tq]


def trunk_layer(x, conv_buf, s0, past_k, past_v, q_start, lam_init, rel_bias,
                norm_mix, w_in, conv_w, a_log, dt_bias, norm_dn, w_o_dn,
                lam_q1, lam_k1, lam_q2, lam_k2, norm_diff, w_o_diff, w_out,
                norm_mlp, w_up, w_down):
    B, T, _ = x.shape
    h = rms_norm(x, norm_mix)
    z = h @ w_in
    (q_dn, k_dn, v_dn, b_dn, a_dn, gate_dn, q_df, k_df, v_df, g_br) = jnp.split(z, SPLIT_POINTS, axis=-1)

    qkv = jnp.concatenate([q_dn, k_dn, v_dn], axis=-1)
    xpad = jnp.concatenate([conv_buf.astype(qkv.dtype), qkv], axis=1)
    conv = xpad[:, 0:T] * conv_w[0]
    for j in range(1, CONV_W):
        conv = conv + xpad[:, j:j + T] * conv_w[j]
    conv = jax.nn.silu(conv)
    new_conv_buf = xpad[:, T:]
    qc, kc, vc = jnp.split(conv, (H_DN * DN_DK, 2 * H_DN * DN_DK), axis=-1)
    qc = l2_normalize(qc.reshape(B, T, H_DN, DN_DK)) * (DN_DK ** -0.5)
    kc = l2_normalize(kc.reshape(B, T, H_DN, DN_DK))
    vc = vc.reshape(B, T, H_DN, DN_DV).astype(F32)
    beta = jax.nn.sigmoid(b_dn.astype(F32))
    g = -jnp.exp(a_log.astype(F32)) * jax.nn.softplus(a_dn.astype(F32) + dt_bias.astype(F32))
    o_dn, s_new = gated_delta_chunked(qc, kc, vc, beta, g, s0.astype(F32))
    o_dn = rms_norm(o_dn, norm_dn) * jax.nn.silu(gate_dn.astype(F32).reshape(B, T, H_DN, DN_DV))
    branch_dn = o_dn.reshape(B, T, H_DN * DN_DV).astype(x.dtype) @ w_o_dn

    k_new = k_df.reshape(B, T, H_DIFF, 2 * DIFF_DQ)
    v_new = v_df.reshape(B, T, H_DIFF, DIFF_DV)
    if past_k is None:
        k_all, v_all = k_new, v_new
    else:
        k_all = jnp.concatenate([past_k.astype(k_new.dtype), k_new], axis=1)
        v_all = jnp.concatenate([past_v.astype(v_new.dtype), v_new], axis=1)
    lam = (jnp.exp(jnp.sum(lam_q1.astype(F32) * lam_k1.astype(F32)))
           - jnp.exp(jnp.sum(lam_q2.astype(F32) * lam_k2.astype(F32))) + lam_init)
    o_df = diff_attention(q_df.reshape(B, T, H_DIFF, 2, DIFF_DQ),
                          k_all.reshape(B, k_all.shape[1], H_DIFF, 2, DIFF_DQ),
                          v_all, q_start, lam, rel_bias)
    o_df = rms_norm(o_df, norm_diff) * (1.0 - lam_init)
    branch_df = o_df.reshape(B, T, H_DIFF * DIFF_DV).astype(x.dtype) @ w_o_diff

    g_a, g_b = jnp.split(g_br, 2, axis=-1)
    merged = jax.nn.sigmoid(g_a) * branch_dn + jax.nn.sigmoid(g_b) * branch_df
    x = x + merged @ w_out

    h2 = rms_norm(x, norm_mlp)
    x = x + jnp.square(jax.nn.relu(h2 @ w_up)) @ w_down
    return x, new_conv_buf, s_new.astype(s0.dtype), k_new, v_new


def setup_inputs(seed: int = 0) -> dict:
    key = jax.random.key(seed)
    ks = jax.random.split(key, 32)
    n_pages = PAST_LEN // PAGE_SIZE
    n_used = DEC_BATCH * n_pages
    n_pool = (n_used * 5) // 4

    def nrm(k, shape, s):
        return jax.random.normal(k, shape, F32) * s

    def gain(k, shape):
        return 1.0 + 0.02 * jax.random.normal(k, shape, F32)

    x_prompt = nrm(ks[0], (BATCH, SEQ, D_MODEL), 1.0)
    x_sample = nrm(ks[1], (DEC_BATCH, DEC_SEQ, D_MODEL), 1.0)
    cache_k = nrm(ks[2], (DEPTH, n_pool, PAGE_SIZE, H_DIFF, 2 * DIFF_DQ), 1.0)
    cache_v = nrm(ks[3], (DEPTH, n_pool, PAGE_SIZE, H_DIFF, DIFF_DV), 1.0)
    state_delta = nrm(ks[4], (DEPTH, DEC_BATCH, H_DN, DN_DK, DN_DV), 0.1)
    state_conv = nrm(ks[5], (DEPTH, DEC_BATCH, CONV_W - 1, CONV_DIM), 1.0)
    perm = jax.random.permutation(ks[6], n_pool)
    page_table = perm[:n_used].reshape(DEC_BATCH, n_pages).astype(jnp.int32)
    rel_bias = nrm(ks[7], (NUM_BUCKETS, H_DIFF), 0.5)
    norm_mix = gain(ks[8], (DEPTH, D_MODEL))
    w_in = nrm(ks[9], (DEPTH, D_MODEL, IN_COLS), D_MODEL ** -0.5)
    conv_w = nrm(ks[10], (DEPTH, CONV_W, CONV_DIM), 0.5)
    a_log = jnp.log(jax.random.uniform(ks[11], (DEPTH, H_DN), F32, 1.0, 16.0))
    dt = jnp.exp(jax.random.uniform(ks[12], (DEPTH, H_DN), F32, math.log(1e-3), math.log(1e-1)))
    dt_bias = dt + jnp.log(-jnp.expm1(-dt))
    norm_dn = gain(ks[13], (DEPTH, DN_DV))
    w_o_dn = nrm(ks[14], (DEPTH, H_DN * DN_DV, D_MODEL), (H_DN * DN_DV) ** -0.5)
    lam_q1 = nrm(ks[15], (DEPTH, DIFF_DQ), 0.1)
    lam_k1 = nrm(ks[16], (DEPTH, DIFF_DQ), 0.1)
    lam_q2 = nrm(ks[17], (DEPTH, DIFF_DQ), 0.1)
    lam_k2 = nrm(ks[18], (DEPTH, DIFF_DQ), 0.1)
    norm_diff = gain(ks[19], (DEPTH, DIFF_DV))
    w_o_diff = nrm(ks[20], (DEPTH, H_DIFF * DIFF_DV, D_MODEL), (H_DIFF * DIFF_DV) ** -0.5)
    w_out = nrm(ks[21], (DEPTH, D_MODEL, D_MODEL), D_MODEL ** -0.5)
    norm_mlp = gain(ks[22], (DEPTH, D_MODEL))
    w_up = nrm(ks[23], (DEPTH, D_MODEL, D_FF), D_MODEL ** -0.5)
    w_down = nrm(ks[24], (DEPTH, D_FF, D_MODEL), D_FF ** -0.5)
    norm_final = gain(ks[25], (D_MODEL,))
    return {"x_prompt": x_prompt, "x_sample": x_sample, "cache_k": cache_k, "cache_v": cache_v,
            "state_delta": state_delta, "state_conv": state_conv, "page_table": page_table,
            "rel_bias": rel_bias, "norm_mix": norm_mix, "w_in": w_in, "conv_w": conv_w,
            "a_log": a_log, "dt_bias": dt_bias, "norm_dn": norm_dn, "w_o_dn": w_o_dn,
            "lam_q1": lam_q1, "lam_k1": lam_k1, "lam_q2": lam_q2, "lam_k2": lam_k2,
            "norm_diff": norm_diff, "w_o_diff": w_o_diff, "w_out": w_out, "norm_mlp": norm_mlp,
            "w_up": w_up, "w_down": w_down, "norm_final": norm_final}


def reference(x_prompt, x_sample, cache_k, cache_v, state_delta, state_conv, page_table,
              rel_bias, norm_mix, w_in, conv_w, a_log, dt_bias, norm_dn, w_o_dn,
              lam_q1, lam_k1, lam_q2, lam_k2, norm_diff, w_o_diff, w_out, norm_mlp,
              w_up, w_down, norm_final):
    B = x_prompt.shape[0]
    DB = x_sample.shape[0]
    n_pages = page_table.shape[1]
    past_len = n_pages * cache_k.shape[2]
    xp, xs = x_prompt, x_sample
    kp_l, vp_l, sp_l, cp_l = [], [], [], []
    ks_l, vs_l, ss_l, cs_l = [], [], [], []
    for l in range(DEPTH):
        lam_init = 0.8 - 0.6 * math.exp(-0.3 * l)
        lw = (norm_mix[l], w_in[l], conv_w[l], a_log[l], dt_bias[l], norm_dn[l], w_o_dn[l],
              lam_q1[l], lam_k1[l], lam_q2[l], lam_k2[l], norm_diff[l], w_o_diff[l], w_out[l],
              norm_mlp[l], w_up[l], w_down[l])
        conv0 = jnp.zeros((B, CONV_W - 1, CONV_DIM), x_prompt.dtype)
        s0 = jnp.zeros((B, H_DN, DN_DK, DN_DV), state_delta.dtype)
        xp, cb, sn, kn, vn = trunk_layer(xp, conv0, s0, None, None, 0, lam_init, rel_bias, *lw)
        kp_l.append(kn); vp_l.append(vn); sp_l.append(sn); cp_l.append(cb)
        past_k = cache_k[l][page_table].reshape(DB, past_len, H_DIFF, 2 * DIFF_DQ)
        past_v = cache_v[l][page_table].reshape(DB, past_len, H_DIFF, DIFF_DV)
        xs, cb, sn, kn, vn = trunk_layer(xs, state_conv[l], state_delta[l], past_k, past_v,
                                         past_len, lam_init, rel_bias, *lw)
        ks_l.append(kn); vs_l.append(vn); ss_l.append(sn); cs_l.append(cb)
    y_prompt = rms_norm(xp, norm_final)
    y_sample = rms_norm(xs, norm_final)
    return (y_prompt, y_sample,
            jnp.stack(kp_l), jnp.stack(vp_l), jnp.stack(sp_l), jnp.stack(cp_l),
            jnp.stack(ks_l), jnp.stack(vs_l), jnp.stack(ss_l), jnp.stack(cs_l))
```

```python
import functools
import math

import jax
import jax.numpy as jnp
from jax import lax
from jax.experimental import pallas as pl
from jax.experimental.pallas import tpu as pltpu

F32 = jnp.float32
BF16 = jnp.bfloat16

H_DN = 8
DN_DK = 128
DN_DV = 128
CONV_W = 4
H_DIFF = 8
DIFF_DQ = 64
DIFF_DV = 128
NUM_BUCKETS = 32
MAX_DISTANCE = 128
EPS = 1e-6

LANES = 128
SUBLANES = 8
VMEM_BYTES_V7X = 64 * 1024 * 1024
VMEM_INTERNAL_ALLOWANCE = 12 * 1024 * 1024

_MAX_EXACT = NUM_BUCKETS // 2
_BUCKET_START = tuple(
    k if k <= _MAX_EXACT else math.ceil(
        _MAX_EXACT * (MAX_DISTANCE / _MAX_EXACT) ** ((k - _MAX_EXACT) / (NUM_BUCKETS - _MAX_EXACT)))
    for k in range(NUM_BUCKETS))
_FAR_REL = _BUCKET_START[-1]

_C_QDN, _C_KDN, _C_VDN, _C_GATE = 0, 1024, 2048, 3072
_C_QDF, _C_KDF, _C_VDF = 4096, 5120, 6144
_C_GA = 7168


def _nbytes(shape, dtype):
    return math.prod(shape) * jnp.dtype(dtype).itemsize


def _vmem_limit(pipelined, resident=()):
    need = 2 * sum(_nbytes(s, d) for s, d in pipelined) + sum(_nbytes(s, d) for s, d in resident)
    return int(min(need + VMEM_INTERNAL_ALLOWANCE, VMEM_BYTES_V7X - 4 * 1024 * 1024))


def _dot(a, b):
    return jnp.dot(a.astype(BF16), b.astype(BF16), preferred_element_type=F32)


def _dot_nt(a, b):
    return lax.dot_general(a.astype(BF16), b.astype(BF16), (((1,), (1,)), ((), ())),
                           preferred_element_type=F32)


def _dot_tn(a, b):
    return lax.dot_general(a.astype(BF16), b.astype(BF16), (((0,), (0,)), ((), ())),
                           preferred_element_type=F32)


def _rms(x, gain):
    return x * lax.rsqrt(jnp.mean(x * x, axis=-1, keepdims=True) + EPS) * gain


def _silu(x):
    return x * jax.nn.sigmoid(x)


def _softplus(x):
    return jnp.maximum(x, 0.0) + jnp.log1p(jnp.exp(-jnp.abs(x)))


def _norm_rows_to(x_ref, g_ref, h_ref):
    tm = x_ref.shape[0]
    rows = min(tm, 128)

    def body(i, c):
        sl = pl.ds(pl.multiple_of(i * rows, rows), rows)
        h_ref[sl, :] = _rms(x_ref[sl, :], g_ref[...]).astype(h_ref.dtype)
        return c

    lax.fori_loop(0, tm // rows, body, 0)


def _mm_kernel(*refs, has_gain, has_res):
    it = iter(refs)
    x_ref = next(it)
    g_ref = next(it) if has_gain else None
    w_ref = next(it)
    r_ref = next(it) if has_res else None
    o_ref = next(it)
    if has_gain:
        h_ref = next(it)

        @pl.when(pl.program_id(1) == 0)
        def _():
            _norm_rows_to(x_ref, g_ref, h_ref)

        lhs = h_ref[...]
    else:
        lhs = x_ref[...]
    acc = jnp.dot(lhs, w_ref[...], preferred_element_type=F32)
    if has_res:
        acc = acc + r_ref[...]
    o_ref[...] = acc.astype(o_ref.dtype)


def _mm(x, w, *, gain=None, residual=None, out_dtype=F32, tm=1024, tn=512, name):
    M, K = x.shape
    N = w.shape[1]
    tm, tn = min(tm, M), min(tn, N)
    assert M % tm == 0 and N % tn == 0, (M, N, tm, tn)
    in_specs = [pl.BlockSpec((tm, K), lambda i, j: (i, 0))]
    args = [x]
    blocks = [((tm, K), x.dtype), ((K, tn), w.dtype), ((tm, tn), out_dtype)]
    scratch = []
    if gain is not None:
        in_specs.append(pl.BlockSpec((1, K), lambda i, j: (0, 0)))
        args.append(gain.reshape(1, K))
        scratch.append(pltpu.VMEM((tm, K), BF16))
    in_specs.append(pl.BlockSpec((K, tn), lambda i, j: (0, j)))
    args.append(w)
    if residual is not None:
        in_specs.append(pl.BlockSpec((tm, tn), lambda i, j: (i, j)))
        args.append(residual)
        blocks.append(((tm, tn), residual.dtype))
    return pl.pallas_call(
        functools.partial(_mm_kernel, has_gain=gain is not None, has_res=residual is not None),
        grid=(M // tm, N // tn),
        in_specs=in_specs,
        out_specs=pl.BlockSpec((tm, tn), lambda i, j: (i, j)),
        out_shape=jax.ShapeDtypeStruct((M, N), out_dtype),
        scratch_shapes=scratch,
        compiler_params=pltpu.CompilerParams(
            dimension_semantics=("parallel", "arbitrary"),
            vmem_limit_bytes=_vmem_limit(blocks, [((tm, K), BF16)] if gain is not None else [])),
        name=name,
    )(*args)


def _merge_kernel(odn_ref, odf_ref, ga_ref, gb_ref, wdn_ref, wdf_ref, o_ref):
    a = jnp.dot(odn_ref[...], wdn_ref[...], preferred_element_type=F32)
    b = jnp.dot(odf_ref[...], wdf_ref[...], preferred_element_type=F32)
    o_ref[...] = (jax.nn.sigmoid(ga_ref[...]) * a + jax.nn.sigmoid(gb_ref[...]) * b).astype(o_ref.dtype)


def _merge(o_dn, o_df, z, w_dn, w_df, *, tm=1024, tn=512, name):
    M, K = o_dn.shape
    N = w_dn.shape[1]
    tm, tn = min(tm, M), min(tn, N)
    assert M % tm == 0 and N % tn == 0 and _C_GA % tn == 0
    ga0 = _C_GA // tn
    gb0 = (_C_GA + N) // tn
    blocks = [((tm, K), BF16)] * 2 + [((tm, tn), F32)] * 2 + [((K, tn), BF16)] * 2 + [((tm, tn), BF16)]
    return pl.pallas_call(
        _merge_kernel,
        grid=(M // tm, N // tn),
        in_specs=[
            pl.BlockSpec((tm, K), lambda i, j: (i, 0)),
            pl.BlockSpec((tm, K), lambda i, j: (i, 0)),
            pl.BlockSpec((tm, tn), lambda i, j: (i, ga0 + j)),
            pl.BlockSpec((tm, tn), lambda i, j: (i, gb0 + j)),
            pl.BlockSpec((K, tn), lambda i, j: (0, j)),
            pl.BlockSpec((K, tn), lambda i, j: (0, j)),
        ],
        out_specs=pl.BlockSpec((tm, tn), lambda i, j: (i, j)),
        out_shape=jax.ShapeDtypeStruct((M, N), BF16),
        compiler_params=pltpu.CompilerParams(
            dimension_semantics=("parallel", "arbitrary"), vmem_limit_bytes=_vmem_limit(blocks)),
        name=name,
    )(o_dn, o_df, z, z, w_dn, w_df)


def _mlp_kernel(*refs, has_final):
    if has_final:
        x_ref, g_ref, wu_ref, wd_ref, gf_ref, o_ref, h_ref, acc_ref = refs
    else:
        x_ref, g_ref, wu_ref, wd_ref, o_ref, h_ref, acc_ref = refs
        gf_ref = None
    f = pl.program_id(1)

    @pl.when(f == 0)
    def _():
        _norm_rows_to(x_ref, g_ref, h_ref)
        acc_ref[...] = jnp.zeros_like(acc_ref)

    u = jnp.dot(h_ref[...], wu_ref[...], preferred_element_type=F32)
    u = jnp.square(jnp.maximum(u, 0.0)).astype(BF16)
    acc_ref[...] += jnp.dot(u, wd_ref[...], preferred_element_type=F32)

    @pl.when(f == pl.num_programs(1) - 1)
    def _():
        y = x_ref[...] + acc_ref[...]
        if has_final:
            y = _rms(y, gf_ref[...])
        o_ref[...] = y


def _mlp(x, gain, w_up, w_down, *, final_gain=None, tm=512, tf=512, name):
    M, D = x.shape
    FF = w_up.shape[1]
    tm, tf = min(tm, M), min(tf, FF)
    assert M % tm == 0 and FF % tf == 0
    in_specs = [
        pl.BlockSpec((tm, D), lambda i, f: (i, 0)),
        pl.BlockSpec((1, D), lambda i, f: (0, 0)),
        pl.BlockSpec((D, tf), lambda i, f: (0, f)),
        pl.BlockSpec((tf, D), lambda i, f: (f, 0)),
    ]
    args = [x, gain.reshape(1, D), w_up, w_down]
    if final_gain is not None:
        in_specs.append(pl.BlockSpec((1, D), lambda i, f: (0, 0)))
        args.append(final_gain.reshape(1, D))
    blocks = [((tm, D), F32), ((D, tf), BF16), ((tf, D), BF16), ((tm, D), F32)]
    resident = [((tm, D), BF16), ((tm, D), F32)]
    return pl.pallas_call(
        functools.partial(_mlp_kernel, has_final=final_gain is not None),
        grid=(M // tm, FF // tf),
        in_specs=in_specs,
        out_specs=pl.BlockSpec((tm, D), lambda i, f: (i, 0)),
        out_shape=jax.ShapeDtypeStruct((M, D), F32),
        scratch_shapes=[pltpu.VMEM((tm, D), BF16), pltpu.VMEM((tm, D), F32)],
        compiler_params=pltpu.CompilerParams(
            dimension_semantics=("parallel", "arbitrary"), vmem_limit_bytes=_vmem_limit(blocks, resident)),
        name=name,
    )(*args)


def _bias_of_rel(rb_ref, h, rel):
    far = rb_ref[NUM_BUCKETS - 1, h]
    v = jnp.full(rel.shape, rb_ref[0, h] - far, F32)
    for k in range(1, NUM_BUCKETS):
        v = jnp.where(rel >= _BUCKET_START[k], rb_ref[k, h] - far, v)
    return v


def _bias_tiles_kernel(rb_ref, diag_ref, sub_ref, *, tk):
    h = pl.program_id(0)
    shape = diag_ref.shape
    rel = lax.broadcasted_iota(jnp.int32, shape, 0) - lax.broadcasted_iota(jnp.int32, shape, 1)
    diag_ref[...] = jnp.where(rel >= 0, _bias_of_rel(rb_ref, h, rel), -jnp.inf)
    sub_ref[...] = _bias_of_rel(rb_ref, h, rel + tk)


def _bias_tiles(rel_bias, tq, tk):
    assert tq == tk and tk + 1 >= _FAR_REL
    H = rel_bias.shape[1]
    return pl.pallas_call(
        functools.partial(_bias_tiles_kernel, tk=tk),
        grid=(H,),
        in_specs=[pl.BlockSpec(memory_space=pltpu.SMEM)],
        out_specs=[pl.BlockSpec((None, tq, tk), lambda h: (h, 0, 0))] * 2,
        out_shape=[jax.ShapeDtypeStruct((H, tq, tk), F32)] * 2,
        name="t5_bias_tiles",
    )(rel_bias)


def _dec_bias_kernel(rb_ref, o_ref, *, page):
    shape = o_ref.shape[1:]
    row = lax.broadcasted_iota(jnp.int32, shape, 0)
    rel = page - lax.broadcasted_iota(jnp.int32, shape, 1)
    last = jnp.zeros(shape, F32)
    new = jnp.zeros(shape, F32)
    for h in range(H_DIFF):
        sel = (row >> 1) == h
        last = jnp.where(sel, _bias_of_rel(rb_ref, h, rel), last)
        new = jnp.where(sel, rb_ref[0, h] - rb_ref[NUM_BUCKETS - 1, h], new)
    o_ref[0] = last
    o_ref[1] = new


def _dec_bias(rel_bias, page):
    assert page + 1 >= _FAR_REL and page <= LANES
    return pl.pallas_call(
        functools.partial(_dec_bias_kernel, page=page),
        in_specs=[pl.BlockSpec(memory_space=pltpu.SMEM)],
        out_specs=pl.BlockSpec(memory_space=pltpu.VMEM),
        out_shape=jax.ShapeDtypeStruct((2, 2 * H_DIFF, page), F32),
        name="t5_bias_decode",
    )(rel_bias)


def _unit_lower_inverse(L, C):
    ri = lax.broadcasted_iota(jnp.int32, (C, C), 0)
    ci = lax.broadcasted_iota(jnp.int32, (C, C), 1)
    eye = (ri == ci).astype(F32)
    T = None
    lb = 0
    while (1 << lb) < C:
        below = ((ri >> (lb + 1)) == (ci >> (lb + 1))) & (((ri >> lb) & 1) == 1) & (((ci >> lb) & 1) == 0)
        Lb = jnp.where(below, L, 0.0)
        T = eye - Lb if T is None else T - _dot(_dot(T, Lb), T)
        lb += 1
    return T


def _delta_chunk(q, k, v, beta, g, S, C):
    ri = lax.broadcasted_iota(jnp.int32, (C, C), 0)
    ci = lax.broadcasted_iota(jnp.int32, (C, C), 1)
    tril = ri >= ci
    g_cols = jnp.broadcast_to(g, (C, C))
    g_row = jnp.sum(jnp.where(ri == ci, g_cols, 0.0), axis=0, keepdims=True)
    cum_col = jnp.sum(jnp.where(tril, jnp.broadcast_to(g_row, (C, C)), 0.0), axis=1, keepdims=True)
    cum_row = jnp.sum(jnp.where(ri <= ci, g_cols, 0.0), axis=0, keepdims=True)
    decay = jnp.where(tril, jnp.exp(jnp.where(tril, cum_col - cum_row, 0.0)), 0.0)
    g_last = jnp.sum(g, axis=0, keepdims=True)
    e_col = jnp.exp(cum_col)

    kb = k * beta
    vb = v * beta
    L = jnp.where(ri > ci, _dot_nt(kb, k) * decay, 0.0)
    T = _unit_lower_inverse(L, C)
    vk = _dot(T, jnp.concatenate([vb, kb * e_col], axis=1))
    value, kcum = vk[:, :DN_DV], vk[:, DN_DV:]
    qk = _dot_nt(q, k) * decay
    r = _dot(jnp.concatenate([kcum, q * e_col], axis=0), S)
    v_new = value - r[:C]
    o = r[C:] + _dot(qk, v_new)
    S_new = S * jnp.exp(g_last) + _dot_tn(k * jnp.exp(g_last - cum_col), v_new)
    return o, S_new


def _delta_kernel(zq_ref, zk_ref, zv_ref, zg_ref, ba_ref, wq_ref, wk_ref, wv_ref, alog_ref, dtb_ref,
                  gain_ref, o_ref, sout_ref, xq_s, xk_s, xv_s, q_s, k_s, v_s, bt_s, gt_s, S_s, *, HG, TT, C):
    hgi = pl.program_id(1)
    t = pl.program_id(2)
    W = HG * LANES

    @pl.when(t == 0)
    def _():
        for x_s in (xq_s, xk_s, xv_s):
            x_s[0:SUBLANES, :] = jnp.zeros((SUBLANES, W), F32)
        S_s[...] = jnp.zeros_like(S_s)

    for z_ref, x_s, w_ref, dst in ((zq_ref, xq_s, wq_ref, q_s), (zk_ref, xk_s, wk_ref, k_s),
                                   (zv_ref, xv_s, wv_ref, v_s)):
        x_s[SUBLANES:SUBLANES + TT, :] = z_ref[...]
        base = SUBLANES - (CONV_W - 1)
        acc = x_s[base:base + TT, :] * w_ref[0:1, :]
        for j in range(1, CONV_W):
            acc = acc + x_s[base + j:base + j + TT, :] * w_ref[j:j + 1, :]
        dst[...] = _silu(acc)
        x_s[0:SUBLANES, :] = x_s[TT:TT + SUBLANES, :]

    for hh in range(HG):
        lanes = slice(hh * LANES, (hh + 1) * LANES)
        qh = q_s[:, lanes]
        q_s[:, lanes] = qh * lax.rsqrt(jnp.sum(qh * qh, axis=-1, keepdims=True) + EPS) * (DN_DK ** -0.5)
        kh = k_s[:, lanes]
        k_s[:, lanes] = kh * lax.rsqrt(jnp.sum(kh * kh, axis=-1, keepdims=True) + EPS)

    ba = ba_ref[...]
    bt_s[...] = jax.nn.sigmoid(ba)
    gt_s[...] = -jnp.exp(alog_ref[...]) * _softplus(ba + dtb_ref[...])

    lane = lax.broadcasted_iota(jnp.int32, (C, LANES), 1)

    def chunk(c, carry):
        sl = pl.ds(pl.multiple_of(c * C, C), C)
        bt = bt_s[sl, :]
        gt = gt_s[sl, :]
        for hh in range(HG):
            lanes = slice(hh * LANES, (hh + 1) * LANES)
            head = hgi * HG + hh
            beta = jnp.sum(jnp.where(lane == head, bt, 0.0), axis=-1, keepdims=True)
            g = jnp.sum(jnp.where(lane == head + H_DN, gt, 0.0), axis=-1, keepdims=True)
            o, S_new = _delta_chunk(q_s[sl, lanes], k_s[sl, lanes], v_s[sl, lanes], beta, g, S_s[hh], C)
            S_s[hh] = S_new
            o_ref[sl, lanes] = (_rms(o, gain_ref[...]) * _silu(zg_ref[sl, lanes])).astype(o_ref.dtype)
        return carry

    lax.fori_loop(0, TT // C, chunk, 0)

    @pl.when(t == pl.num_programs(2) - 1)
    def _():
        sout_ref[...] = S_s[...]


def _delta_prompt(z, ba, conv_w, alog_row, dtb_row, gain_dn, B, T, *, HG=4, TT=512, C=64, name):
    M = z.shape[0]
    TT = min(TT, T)
    C = min(C, TT)
    assert T % TT == 0 and TT % C == 0 and H_DN % HG == 0 and C & (C - 1) == 0 and C % 16 == 0
    NT = T // TT
    W = HG * LANES
    nhg = H_DN // HG

    def zspec(col0):
        return pl.BlockSpec((TT, W), lambda b, hg, t: (b * NT + t, col0 // W + hg))

    def wspec(col0):
        return pl.BlockSpec((CONV_W, W), lambda b, hg, t: (0, col0 // W + hg))

    row = pl.BlockSpec((1, LANES), lambda b, hg, t: (0, 0))
    blocks = [((TT, W), F32)] * 4 + [((TT, LANES), F32)] + [((TT, W), BF16), ((HG, DN_DK, DN_DV), F32)]
    resident = [((TT + SUBLANES, W), F32)] * 3 + [((TT, W), F32)] * 3 + [((TT, LANES), F32)] * 2 \
        + [((HG, DN_DK, DN_DV), F32)]
    return pl.pallas_call(
        functools.partial(_delta_kernel, HG=HG, TT=TT, C=C),
        grid=(B, nhg, NT),
        in_specs=[zspec(_C_QDN), zspec(_C_KDN), zspec(_C_VDN), zspec(_C_GATE),
                  pl.BlockSpec((TT, LANES), lambda b, hg, t: (b * NT + t, 0)),
                  wspec(0), wspec(H_DN * DN_DK), wspec(2 * H_DN * DN_DK), row, row, row],
        out_specs=[pl.BlockSpec((TT, W), lambda b, hg, t: (b * NT + t, hg)),
                   pl.BlockSpec((None, HG, DN_DK, DN_DV), lambda b, hg, t: (b, hg, 0, 0))],
        out_shape=[jax.ShapeDtypeStruct((M, H_DN * DN_DV), BF16),
                   jax.ShapeDtypeStruct((B, H_DN, DN_DK, DN_DV), F32)],
        scratch_shapes=[pltpu.VMEM((TT + SUBLANES, W), F32)] * 3 + [pltpu.VMEM((TT, W), F32)] * 3
        + [pltpu.VMEM((TT, LANES), F32)] * 2 + [pltpu.VMEM((HG, DN_DK, DN_DV), F32)],
        compiler_params=pltpu.CompilerParams(
            dimension_semantics=("parallel", "parallel", "arbitrary"),
            vmem_limit_bytes=_vmem_limit(blocks, resident)),
        name=name,
    )(z, z, z, z, ba, conv_w, conv_w, conv_w, alog_row, dtb_row, gain_dn.reshape(1, LANES))


def _lambda(lam_ref, lam_init):
    lv = lam_ref[...]
    s1 = jnp.sum(lv[0:1, :] * lv[1:2, :], axis=-1, keepdims=True)
    s2 = jnp.sum(lv[2:3, :] * lv[3:4, :], axis=-1, keepdims=True)
    return jnp.exp(s1) - jnp.exp(s2) + lam_init


def _attn_kernel(zq_ref, zk_ref, zv_ref, diag_ref, sub_ref, lam_ref, gain_ref, o_ref,
                 kb_s, vb_s, m_s, l_s, acc_s, *, tq, tk, lam_init):
    qi = pl.program_id(2)

    @pl.when(qi == 0)
    def _():
        kb_s[...] = zk_ref[...].astype(BF16)
        vb_s[...] = zv_ref[...].astype(BF16)

    q = zq_ref[...] * (DIFF_DQ ** -0.5)
    lane = lax.broadcasted_iota(jnp.int32, q.shape, 1)
    qm = (jnp.where(lane < DIFF_DQ, q, 0.0).astype(BF16), jnp.where(lane >= DIFF_DQ, q, 0.0).astype(BF16))
    m_s[...] = jnp.full(m_s.shape, -jnp.inf, F32)
    l_s[...] = jnp.zeros_like(l_s)
    acc_s[...] = jnp.zeros_like(acc_s)

    def block(j, bias):
        sl = pl.ds(pl.multiple_of(j * tk, tk), tk)
        ks = kb_s[sl, :]
        vs = vb_s[sl, :]
        for mi in range(2):
            s = lax.dot_general(qm[mi], ks, (((1,), (1,)), ((), ())), preferred_element_type=F32)
            if bias is not None:
                s = s + bias
            m_old = m_s[mi]
            m_new = jnp.maximum(m_old, jnp.max(s, axis=-1, keepdims=True))
            alpha = jnp.exp(m_old - m_new)
            p = jnp.exp(s - m_new)
            l_s[mi] = alpha * l_s[mi] + jnp.sum(p, axis=-1, keepdims=True)
            acc_s[mi] = alpha * acc_s[mi] + jnp.dot(p.astype(BF16), vs, preferred_element_type=F32)
            m_s[mi] = m_new

    def far(j, c):
        block(j, None)
        return c

    lax.fori_loop(0, jnp.maximum(qi - 1, 0), far, 0)

    @pl.when(qi > 0)
    def _():
        block(qi - 1, sub_ref[...])

    block(qi, diag_ref[...])

    lam = _lambda(lam_ref, lam_init)
    o = acc_s[0] / l_s[0] - lam * (acc_s[1] / l_s[1])
    o_ref[...] = (_rms(o, gain_ref[...]) * (1.0 - lam_init)).astype(o_ref.dtype)


def _attn_prompt(z, diag, sub, lam_rows, gain_df, B, T, lam_init, *, tq, name):
    M = z.shape[0]
    tk = tq
    assert T % tq == 0
    NQ = T // tq
    row = pl.BlockSpec((1, LANES), lambda b, h, qi: (0, 0))
    blocks = [((tq, LANES), F32), ((T, LANES), F32), ((T, LANES), F32), ((tq, tk), F32), ((tq, tk), F32),
              ((tq, LANES), BF16)]
    resident = [((T, LANES), BF16)] * 2 + [((2, tq, LANES), F32)] * 3
    return pl.pallas_call(
        functools.partial(_attn_kernel, tq=tq, tk=tk, lam_init=lam_init),
        grid=(B, H_DIFF, NQ),
        in_specs=[pl.BlockSpec((tq, LANES), lambda b, h, qi: (b * NQ + qi, _C_QDF // LANES + h)),
                  pl.BlockSpec((T, LANES), lambda b, h, qi: (b, _C_KDF // LANES + h)),
                  pl.BlockSpec((T, LANES), lambda b, h, qi: (b, _C_VDF // LANES + h)),
                  pl.BlockSpec((None, tq, tk), lambda b, h, qi: (h, 0, 0)),
                  pl.BlockSpec((None, tq, tk), lambda b, h, qi: (h, 0, 0)),
                  pl.BlockSpec((SUBLANES, LANES), lambda b, h, qi: (0, 0)), row],
        out_specs=pl.BlockSpec((tq, LANES), lambda b, h, qi: (b * NQ + qi, h)),
        out_shape=jax.ShapeDtypeStruct((M, H_DIFF * DIFF_DV), BF16),
        scratch_shapes=[pltpu.VMEM((T, LANES), BF16)] * 2
        + [pltpu.VMEM((2, tq, 1), F32)] * 2 + [pltpu.VMEM((2, tq, LANES), F32)],
        compiler_params=pltpu.CompilerParams(
            dimension_semantics=("parallel", "parallel", "arbitrary"),
            vmem_limit_bytes=_vmem_limit(blocks, resident)),
        name=name,
    )(z, z, z, diag, sub, lam_rows, gain_df.reshape(1, LANES))


def _delta_step_kernel(zq_ref, zk_ref, zv_ref, zg_ref, ba_ref, cq_ref, ck_ref, cv_ref, wq_ref, wk_ref, wv_ref,
                       alog_ref, dtb_ref, gain_ref, s_ref, o_ref, sout_ref):
    def conv(z_ref, c_ref, w_ref):
        acc = z_ref[...] * w_ref[CONV_W - 1:CONV_W, :]
        for j in range(CONV_W - 1):
            acc = acc + c_ref[j:j + 1, :] * w_ref[j:j + 1, :]
        return _silu(acc)

    cq, ck, cv = conv(zq_ref, cq_ref, wq_ref), conv(zk_ref, ck_ref, wk_ref), conv(zv_ref, cv_ref, wv_ref)
    ba = ba_ref[...]
    bt = jax.nn.sigmoid(ba)
    gt = -jnp.exp(alog_ref[...]) * _softplus(ba + dtb_ref[...])
    lane = lax.broadcasted_iota(jnp.int32, (1, LANES), 1)
    pad = jnp.zeros((SUBLANES - 2, LANES), F32)
    for h in range(H_DN):
        lanes = slice(h * LANES, (h + 1) * LANES)
        q = cq[:, lanes]
        q = q * lax.rsqrt(jnp.sum(q * q, axis=-1, keepdims=True) + EPS) * (DN_DK ** -0.5)
        k = ck[:, lanes]
        k = k * lax.rsqrt(jnp.sum(k * k, axis=-1, keepdims=True) + EPS)
        v = cv[:, lanes]
        beta = jnp.sum(jnp.where(lane == h, bt, 0.0), axis=-1, keepdims=True)
        eg = jnp.exp(jnp.sum(jnp.where(lane == h + H_DN, gt, 0.0), axis=-1, keepdims=True))
        S = s_ref[h]
        r = _dot(jnp.concatenate([k, q, pad], axis=0), S)
        v_new = beta * (v - eg * r[0:1])
        o = eg * r[1:2] + jnp.sum(q * k, axis=-1, keepdims=True) * v_new
        zeros7 = jnp.zeros((SUBLANES - 1, LANES), F32)
        outer = _dot_tn(jnp.concatenate([k, zeros7], axis=0), jnp.concatenate([v_new, zeros7], axis=0))
        sout_ref[h] = S * eg + outer
        o_ref[:, lanes] = (_rms(o, gain_ref[...]) * _silu(zg_ref[:, lanes])).astype(o_ref.dtype)


def _delta_step(z3, ba3, conv_state, conv_w, alog_row, dtb_row, gain_dn, state, layer, *, name):
    DB = z3.shape[0]
    W = H_DN * LANES

    def zspec(col0):
        return pl.BlockSpec((None, 1, W), lambda b: (b, 0, col0 // W))

    def cspec(col0):
        return pl.BlockSpec((None, None, CONV_W - 1, W), lambda b: (layer, b, 0, col0 // W))

    def wspec(col0):
        return pl.BlockSpec((CONV_W, W), lambda b: (0, col0 // W))

    row = pl.BlockSpec((1, LANES), lambda b: (0, 0))
    return pl.pallas_call(
        _delta_step_kernel,
        grid=(DB,),
        in_specs=[zspec(_C_QDN), zspec(_C_KDN), zspec(_C_VDN), zspec(_C_GATE),
                  pl.BlockSpec((None, 1, LANES), lambda b: (b, 0, 0)),
                  cspec(0), cspec(W), cspec(2 * W), wspec(0), wspec(W), wspec(2 * W), row, row, row,
                  pl.BlockSpec((None, None, H_DN, DN_DK, DN_DV), lambda b: (layer, b, 0, 0, 0))],
        out_specs=[pl.BlockSpec((None, 1, W), lambda b: (b, 0, 0)),
                   pl.BlockSpec((None, H_DN, DN_DK, DN_DV), lambda b: (b, 0, 0, 0))],
        out_shape=[jax.ShapeDtypeStruct((DB, 1, W), F32),
                   jax.ShapeDtypeStruct((DB, H_DN, DN_DK, DN_DV), F32)],
        compiler_params=pltpu.CompilerParams(dimension_semantics=("parallel",)),
        name=name,
    )(z3, z3, z3, z3, ba3, conv_state, conv_state, conv_state, conv_w, conv_w, conv_w,
      alog_row, dtb_row, gain_dn.reshape(1, LANES), state)


def _decode_kernel(pt_ref, qt_ref, knew_ref, vnew_ref, bias_ref, lam_ref, gain_ref, *rest, G, n_pages, lam_init):
    k_refs = rest[:G]
    v_refs = rest[G:2 * G]
    o_ref = rest[2 * G]
    m_s, l_s, acc_s = rest[2 * G + 1:]
    step = pl.program_id(1)
    nsteps = pl.num_programs(1)
    R = 2 * H_DIFF

    @pl.when(step == 0)
    def _():
        m_s[...] = jnp.full(m_s.shape, -jnp.inf, F32)
        l_s[...] = jnp.zeros_like(l_s)
        acc_s[...] = jnp.zeros_like(acc_s)

    qt = qt_ref[...]
    scores = []
    for i in range(G):
        s = lax.dot_general(qt, k_refs[i][...].astype(BF16), (((1,), (1,)), ((), ())),
                            preferred_element_type=F32)
        if i == G - 1:
            s = s + jnp.where(step == nsteps - 1, bias_ref[0], 0.0)
        scores.append(s)
    m_old = m_s[...]
    m_new = m_old
    for s in scores:
        m_new = jnp.maximum(m_new, jnp.max(s, axis=-1, keepdims=True))
    alpha = jnp.exp(m_old - m_new)
    l_new = alpha * l_s[...]
    acc = alpha * acc_s[...]
    for i in range(G):
        p = jnp.exp(scores[i] - m_new)
        l_new = l_new + jnp.sum(p, axis=-1, keepdims=True)
        acc = acc + jnp.dot(p.astype(BF16), v_refs[i][...].astype(BF16), preferred_element_type=F32)
    m_s[...] = m_new
    l_s[...] = l_new
    acc_s[...] = acc

    @pl.when(step == nsteps - 1)
    def _():
        kn = knew_ref[...].astype(BF16).astype(F32)
        s = jnp.sum(qt.astype(F32) * kn, axis=-1, keepdims=True) + bias_ref[1][:, 0:1]
        m_o = m_s[...]
        m_n = jnp.maximum(m_o, s)
        a = jnp.exp(m_o - m_n)
        p = jnp.exp(s - m_n)
        l_f = a * l_s[...] + p
        acc_f = (a * acc_s[...] + p * vnew_ref[...]) / l_f
        lam = _lambda(lam_ref, lam_init)
        for h in range(H_DIFF):
            lanes = slice(h * LANES, (h + 1) * LANES)
            o = acc_f[2 * h:2 * h + 1, lanes] - lam * acc_f[2 * h + 1:2 * h + 2, lanes]
            o_ref[:, lanes] = (_rms(o, gain_ref[...]) * (1.0 - lam_init)).astype(o_ref.dtype)


def _decode_attn(qt, knew3, vnew3, dec_bias, lam_rows, gain_df, cache_k4, cache_v4, page_table, layer,
                 lam_init, *, G=4, name):
    DB, n_pages = page_table.shape
    page, W = cache_k4.shape[2], cache_k4.shape[3]
    R = 2 * H_DIFF
    G = min(G, n_pages)
    assert n_pages % G == 0

    def pspec(i):
        return pl.BlockSpec((None, None, page, W), lambda b, s, pt: (layer, pt[b, s * G + i], 0, 0))

    row = pl.BlockSpec((1, LANES), lambda b, s, pt: (0, 0))
    blocks = [((page, W), F32)] * (2 * G)
    grid_spec = pltpu.PrefetchScalarGridSpec(
        num_scalar_prefetch=1,
        grid=(DB, n_pages // G),
        in_specs=[pl.BlockSpec((None, R, W), lambda b, s, pt: (b, 0, 0)),
                  pl.BlockSpec((None, 1, W), lambda b, s, pt: (b, 0, 0)),
                  pl.BlockSpec((None, 1, W), lambda b, s, pt: (b, 0, 0)),
                  pl.BlockSpec((2, R, page), lambda b, s, pt: (0, 0, 0)),
                  pl.BlockSpec((SUBLANES, LANES), lambda b, s, pt: (0, 0)), row]
        + [pspec(i) for i in range(G)] + [pspec(i) for i in range(G)],
        out_specs=pl.BlockSpec((None, 1, W), lambda b, s, pt: (b, 0, 0)),
        scratch_shapes=[pltpu.VMEM((R, 1), F32), pltpu.VMEM((R, 1), F32), pltpu.VMEM((R, W), F32)],
    )
    return pl.pallas_call(
        functools.partial(_decode_kernel, G=G, n_pages=n_pages, lam_init=lam_init),
        grid_spec=grid_spec,
        out_shape=jax.ShapeDtypeStruct((DB, 1, W), F32),
        compiler_params=pltpu.CompilerParams(
            dimension_semantics=("parallel", "arbitrary"), vmem_limit_bytes=_vmem_limit(blocks)),
        name=name,
    )(page_table, qt, knew3, vnew3, dec_bias, lam_rows, gain_df.reshape(1, LANES),
      *([cache_k4] * G), *([cache_v4] * G))


def _pad_rows(x, rows):
    return jnp.pad(x, ((0, rows - x.shape[0]), (0, 0)))


def kernel(x_prompt, x_sample, cache_k, cache_v, state_delta, state_conv, page_table, rel_bias, norm_mix, w_in, conv_w, a_log, dt_bias, norm_dn, w_o_dn, lam_q1, lam_k1, lam_q2, lam_k2, norm_diff, w_o_diff, w_out, norm_mlp, w_up, w_down, norm_final):
    B, T, D = x_prompt.shape
    DB = x_sample.shape[0]
    depth = w_in.shape[0]
    n_pool, page = cache_k.shape[1], cache_k.shape[2]
    conv_dim = conv_w.shape[-1]
    n_ba = 2 * H_DN
    assert conv_dim == 3 * H_DN * DN_DK and x_sample.shape[1] == 1
    MS = 16
    tq = min(256, T)

    xp = x_prompt.reshape(B * T, D)
    xs = _pad_rows(x_sample.reshape(DB, D), MS)
    diag, sub = _bias_tiles(rel_bias, tq, tq)
    dec_bias = _dec_bias(rel_bias, page)
    ck4 = cache_k.reshape(depth, n_pool, page, H_DIFF * 2 * DIFF_DQ)
    cv4 = cache_v.reshape(depth, n_pool, page, H_DIFF * DIFF_DV)

    col = jnp.arange(H_DIFF * 2 * DIFF_DQ) // DIFF_DQ
    qsel = (col[None, :] == jnp.arange(2 * H_DIFF)[:, None])

    kp_l, vp_l, sp_l, cp_l, ks_l, vs_l, ss_l, cs_l = [], [], [], [], [], [], [], []
    for l in range(depth):
        lam_init = 0.8 - 0.6 * math.exp(-0.3 * l)
        wl = w_in[l]
        w_main = jnp.concatenate([wl[:, :conv_dim], wl[:, conv_dim + n_ba:]], axis=1).astype(BF16)
        w_ba = jnp.pad(wl[:, conv_dim:conv_dim + n_ba], ((0, 0), (0, LANES - n_ba))).astype(BF16)
        w_odn, w_odf, w_o = w_o_dn[l].astype(BF16), w_o_diff[l].astype(BF16), w_out[l].astype(BF16)
        w_u, w_d = w_up[l].astype(BF16), w_down[l].astype(BF16)
        alog_row = jnp.pad(a_log[l], (H_DN, LANES - n_ba)).reshape(1, LANES)
        dtb_row = jnp.pad(dt_bias[l], (H_DN, LANES - n_ba)).reshape(1, LANES)
        lam_rows = jnp.pad(jnp.stack([lam_q1[l], lam_k1[l], lam_q2[l], lam_k2[l]]),
                           ((0, SUBLANES - 4), (0, LANES - DIFF_DQ)))
        last = l == depth - 1

        z = _mm(xp, w_main, gain=norm_mix[l], name=f"in_proj_p{l}")
        ba = _mm(xp, w_ba, gain=norm_mix[l], name=f"in_proj_ba_p{l}")
        o_dn, s_fin = _delta_prompt(z, ba, conv_w[l], alog_row, dtb_row, norm_dn[l], B, T, name=f"delta_p{l}")
        o_df = _attn_prompt(z, diag, sub, lam_rows, norm_diff[l], B, T, lam_init, tq=tq, name=f"attn_p{l}")
        merged = _merge(o_dn, o_df, z, w_odn, w_odf, name=f"merge_p{l}")
        xp = _mm(merged, w_o, residual=xp, name=f"out_proj_p{l}")
        xp = _mlp(xp, norm_mlp[l], w_u, w_d, final_gain=norm_final if last else None, name=f"mlp_p{l}")
        z3 = z.reshape(B, T, -1)
        kp_l.append(z3[:, :, _C_KDF:_C_KDF + 1024].reshape(B, T, H_DIFF, 2 * DIFF_DQ))
        vp_l.append(z3[:, :, _C_VDF:_C_VDF + 1024].reshape(B, T, H_DIFF, DIFF_DV))
        sp_l.append(s_fin)
        cp_l.append(z3[:, T - (CONV_W - 1):, :conv_dim])

        zs = _mm(xs, w_main, gain=norm_mix[l], tn=1024, name=f"in_proj_s{l}")
        bas = _mm(xs, w_ba, gain=norm_mix[l], name=f"in_proj_ba_s{l}")
        zs3 = zs[:DB].reshape(DB, 1, -1)
        o_dn_s, s_new = _delta_step(zs3, bas[:DB].reshape(DB, 1, LANES), state_conv, conv_w[l], alog_row,
                                    dtb_row, norm_dn[l], state_delta, l, name=f"delta_s{l}")
        q_s = zs[:DB, _C_QDF:_C_QDF + 1024] * (DIFF_DQ ** -0.5)
        qt = jnp.where(qsel[None], q_s[:, None, :], 0.0).astype(BF16)
        k_s = zs3[:, :, _C_KDF:_C_KDF + 1024]
        v_s = zs3[:, :, _C_VDF:_C_VDF + 1024]
        o_df_s = _decode_attn(qt, k_s, v_s, dec_bias, lam_rows, norm_diff[l], ck4, cv4, page_table, l,
                              lam_init, name=f"attn_s{l}")
        merged_s = _merge(_pad_rows(o_dn_s.reshape(DB, -1), MS).astype(BF16),
                          _pad_rows(o_df_s.reshape(DB, -1), MS).astype(BF16), zs,
                          w_odn, w_odf, tn=1024, name=f"merge_s{l}")
        xs = _mm(merged_s, w_o, residual=xs, tn=1024, name=f"out_proj_s{l}")
        xs = _mlp(xs, norm_mlp[l], w_u, w_d, final_gain=norm_final if last else None, name=f"mlp_s{l}")
        ks_l.append(k_s.reshape(DB, 1, H_DIFF, 2 * DIFF_DQ))
        vs_l.append(v_s.reshape(DB, 1, H_DIFF, DIFF_DV))
        ss_l.append(s_new)
        cs_l.append(jnp.concatenate([state_conv[l][:, 1:], zs3[:, :, :conv_dim]], axis=1))

    y_prompt = xp.reshape(B, T, D)
    y_sample = xs[:DB].reshape(DB, 1, D)
    return (y_prompt, y_sample,
            jnp.stack(kp_l), jnp.stack(vp_l), jnp.stack(sp_l), jnp.stack(cp_l),
            jnp.stack(ks_l), jnp.stack(vs_l), jnp.stack(ss_l), jnp.stack(cs_l))
```

```python
import functools
import math

import jax
import jax.numpy as jnp
from jax import lax
from jax.experimental import pallas as pl
from jax.experimental.pallas import tpu as pltpu

F32 = jnp.float32
BF16 = jnp.bfloat16

H_DN = 8
DN_DK = 128
DN_DV = 128
CONV_W = 4
H_DIFF = 8
DIFF_DQ = 64
DIFF_DV = 128
NUM_BUCKETS = 32
MAX_DISTANCE = 128
EPS = 1e-6

LANES = 128
SUBLANES = 8
VMEM_BYTES_V7X = 64 * 1024 * 1024
VMEM_INTERNAL_ALLOWANCE = 12 * 1024 * 1024

_MAX_EXACT = NUM_BUCKETS // 2
_BUCKET_START = tuple(
    k if k <= _MAX_EXACT else math.ceil(
        _MAX_EXACT * (MAX_DISTANCE / _MAX_EXACT) ** ((k - _MAX_EXACT) / (NUM_BUCKETS - _MAX_EXACT)))
    for k in range(NUM_BUCKETS))
_FAR_REL = _BUCKET_START[-1]

_C_QDN, _C_KDN, _C_VDN, _C_GATE = 0, 1024, 2048, 3072
_C_QDF, _C_KDF, _C_VDF = 4096, 5120, 6144
_C_GA = 7168


def _nbytes(shape, dtype):
    return math.prod(shape) * jnp.dtype(dtype).itemsize


def _vmem_limit(pipelined, resident=()):
    need = 2 * sum(_nbytes(s, d) for s, d in pipelined) + sum(_nbytes(s, d) for s, d in resident)
    return int(min(need + VMEM_INTERNAL_ALLOWANCE, VMEM_BYTES_V7X - 4 * 1024 * 1024))


def _dot(a, b):
    return jnp.dot(a.astype(BF16), b.astype(BF16), preferred_element_type=F32)


def _dot_nt(a, b):
    return lax.dot_general(a.astype(BF16), b.astype(BF16), (((1,), (1,)), ((), ())),
                           preferred_element_type=F32)


def _dot_tn(a, b):
    return lax.dot_general(a.astype(BF16), b.astype(BF16), (((0,), (0,)), ((), ())),
                           preferred_element_type=F32)


def _bdot(a, b):
    return lax.dot_general(a.astype(BF16), b.astype(BF16), (((2,), (1,)), ((0,), (0,))),
                           preferred_element_type=F32)


def _bdot_nt(a, b):
    return lax.dot_general(a.astype(BF16), b.astype(BF16), (((2,), (2,)), ((0,), (0,))),
                           preferred_element_type=F32)


def _bdot_tn(a, b):
    return lax.dot_general(a.astype(BF16), b.astype(BF16), (((1,), (1,)), ((0,), (0,))),
                           preferred_element_type=F32)


def _rms(x, gain):
    return x * lax.rsqrt(jnp.mean(x * x, axis=-1, keepdims=True) + EPS) * gain


def _silu(x):
    return x * jax.nn.sigmoid(x)


def _softplus(x):
    return jnp.maximum(x, 0.0) + jnp.log1p(jnp.exp(-jnp.abs(x)))


def _norm_rows_to(x_ref, g_ref, h_ref):
    tm = x_ref.shape[0]
    rows = min(tm, 128)

    def body(i, c):
        sl = pl.ds(pl.multiple_of(i * rows, rows), rows)
        h_ref[sl, :] = _rms(x_ref[sl, :], g_ref[...]).astype(h_ref.dtype)
        return c

    lax.fori_loop(0, tm // rows, body, 0)


def _mm_kernel(*refs, has_gain, has_res):
    it = iter(refs)
    x_ref = next(it)
    g_ref = next(it) if has_gain else None
    w_ref = next(it)
    r_ref = next(it) if has_res else None
    o_ref = next(it)
    if has_gain:
        h_ref = next(it)

        @pl.when(pl.program_id(1) == 0)
        def _():
            _norm_rows_to(x_ref, g_ref, h_ref)

        lhs = h_ref[...]
    else:
        lhs = x_ref[...]
    acc = jnp.dot(lhs, w_ref[...], preferred_element_type=F32)
    if has_res:
        acc = acc + r_ref[...]
    o_ref[...] = acc.astype(o_ref.dtype)


def _mm(x, w, *, gain=None, residual=None, out_dtype=F32, tm=1024, tn=512, name):
    M, K = x.shape
    N = w.shape[1]
    tm, tn = min(tm, M), min(tn, N)
    assert M % tm == 0 and N % tn == 0, (M, N, tm, tn)
    in_specs = [pl.BlockSpec((tm, K), lambda i, j: (i, 0))]
    args = [x]
    blocks = [((tm, K), x.dtype), ((K, tn), w.dtype), ((tm, tn), out_dtype)]
    scratch = []
    if gain is not None:
        in_specs.append(pl.BlockSpec((1, K), lambda i, j: (0, 0)))
        args.append(gain.reshape(1, K))
        scratch.append(pltpu.VMEM((tm, K), BF16))
    in_specs.append(pl.BlockSpec((K, tn), lambda i, j: (0, j)))
    args.append(w)
    if residual is not None:
        in_specs.append(pl.BlockSpec((tm, tn), lambda i, j: (i, j)))
        args.append(residual)
        blocks.append(((tm, tn), residual.dtype))
    return pl.pallas_call(
        functools.partial(_mm_kernel, has_gain=gain is not None, has_res=residual is not None),
        grid=(M // tm, N // tn),
        in_specs=in_specs,
        out_specs=pl.BlockSpec((tm, tn), lambda i, j: (i, j)),
        out_shape=jax.ShapeDtypeStruct((M, N), out_dtype),
        scratch_shapes=scratch,
        compiler_params=pltpu.CompilerParams(
            dimension_semantics=("parallel", "arbitrary"),
            vmem_limit_bytes=_vmem_limit(blocks, [((tm, K), BF16)] if gain is not None else [])),
        name=name,
    )(*args)


def _merge_kernel(odn_ref, odf_ref, ga_ref, gb_ref, wdn_ref, wdf_ref, o_ref):
    a = jnp.dot(odn_ref[...], wdn_ref[...], preferred_element_type=F32)
    b = jnp.dot(odf_ref[...], wdf_ref[...], preferred_element_type=F32)
    o_ref[...] = (jax.nn.sigmoid(ga_ref[...]) * a + jax.nn.sigmoid(gb_ref[...]) * b).astype(o_ref.dtype)


def _merge(o_dn, o_df, z, w_dn, w_df, *, tm=1024, tn=512, name):
    M, K = o_dn.shape
    N = w_dn.shape[1]
    tm, tn = min(tm, M), min(tn, N)
    assert M % tm == 0 and N % tn == 0 and _C_GA % tn == 0
    ga0 = _C_GA // tn
    gb0 = (_C_GA + N) // tn
    blocks = [((tm, K), BF16)] * 2 + [((tm, tn), F32)] * 2 + [((K, tn), BF16)] * 2 + [((tm, tn), BF16)]
    return pl.pallas_call(
        _merge_kernel,
        grid=(M // tm, N // tn),
        in_specs=[
            pl.BlockSpec((tm, K), lambda i, j: (i, 0)),
            pl.BlockSpec((tm, K), lambda i, j: (i, 0)),
            pl.BlockSpec((tm, tn), lambda i, j: (i, ga0 + j)),
            pl.BlockSpec((tm, tn), lambda i, j: (i, gb0 + j)),
            pl.BlockSpec((K, tn), lambda i, j: (0, j)),
            pl.BlockSpec((K, tn), lambda i, j: (0, j)),
        ],
        out_specs=pl.BlockSpec((tm, tn), lambda i, j: (i, j)),
        out_shape=jax.ShapeDtypeStruct((M, N), BF16),
        compiler_params=pltpu.CompilerParams(
            dimension_semantics=("parallel", "arbitrary"), vmem_limit_bytes=_vmem_limit(blocks)),
        name=name,
    )(o_dn, o_df, z, z, w_dn, w_df)


def _mlp_kernel(*refs, has_final):
    if has_final:
        x_ref, g_ref, wu_ref, wd_ref, gf_ref, o_ref, h_ref, acc_ref = refs
    else:
        x_ref, g_ref, wu_ref, wd_ref, o_ref, h_ref, acc_ref = refs
        gf_ref = None
    f = pl.program_id(1)

    @pl.when(f == 0)
    def _():
        _norm_rows_to(x_ref, g_ref, h_ref)
        acc_ref[...] = jnp.zeros_like(acc_ref)

    u = jnp.dot(h_ref[...], wu_ref[...], preferred_element_type=F32)
    u = jnp.square(jnp.maximum(u, 0.0)).astype(BF16)
    acc_ref[...] += jnp.dot(u, wd_ref[...], preferred_element_type=F32)

    @pl.when(f == pl.num_programs(1) - 1)
    def _():
        y = x_ref[...] + acc_ref[...]
        if has_final:
            y = _rms(y, gf_ref[...])
        o_ref[...] = y


def _mlp(x, gain, w_up, w_down, *, final_gain=None, tm=512, tf=1024, name):
    M, D = x.shape
    FF = w_up.shape[1]
    tm, tf = min(tm, M), min(tf, FF)
    assert M % tm == 0 and FF % tf == 0
    in_specs = [
        pl.BlockSpec((tm, D), lambda i, f: (i, 0)),
        pl.BlockSpec((1, D), lambda i, f: (0, 0)),
        pl.BlockSpec((D, tf), lambda i, f: (0, f)),
        pl.BlockSpec((tf, D), lambda i, f: (f, 0)),
    ]
    args = [x, gain.reshape(1, D), w_up, w_down]
    if final_gain is not None:
        in_specs.append(pl.BlockSpec((1, D), lambda i, f: (0, 0)))
        args.append(final_gain.reshape(1, D))
    blocks = [((tm, D), F32), ((D, tf), BF16), ((tf, D), BF16), ((tm, D), F32)]
    resident = [((tm, D), BF16), ((tm, D), F32)]
    return pl.pallas_call(
        functools.partial(_mlp_kernel, has_final=final_gain is not None),
        grid=(M // tm, FF // tf),
        in_specs=in_specs,
        out_specs=pl.BlockSpec((tm, D), lambda i, f: (i, 0)),
        out_shape=jax.ShapeDtypeStruct((M, D), F32),
        scratch_shapes=[pltpu.VMEM((tm, D), BF16), pltpu.VMEM((tm, D), F32)],
        compiler_params=pltpu.CompilerParams(
            dimension_semantics=("parallel", "arbitrary"), vmem_limit_bytes=_vmem_limit(blocks, resident)),
        name=name,
    )(*args)


def _bias_of_rel(rb_ref, h, rel):
    far = rb_ref[NUM_BUCKETS - 1, h]
    v = jnp.full(rel.shape, rb_ref[0, h] - far, F32)
    for k in range(1, NUM_BUCKETS):
        v = jnp.where(rel >= _BUCKET_START[k], rb_ref[k, h] - far, v)
    return v


def _bias_tiles_kernel(rb_ref, diag_ref, sub_ref, *, tk):
    h = pl.program_id(0)
    shape = diag_ref.shape
    rel = lax.broadcasted_iota(jnp.int32, shape, 0) - lax.broadcasted_iota(jnp.int32, shape, 1)
    diag_ref[...] = jnp.where(rel >= 0, _bias_of_rel(rb_ref, h, rel), -jnp.inf)
    sub_ref[...] = _bias_of_rel(rb_ref, h, rel + tk)


def _bias_tiles(rel_bias, tq, tk):
    assert tq == tk and tk + 1 >= _FAR_REL
    H = rel_bias.shape[1]
    return pl.pallas_call(
        functools.partial(_bias_tiles_kernel, tk=tk),
        grid=(H,),
        in_specs=[pl.BlockSpec(memory_space=pltpu.SMEM)],
        out_specs=[pl.BlockSpec((None, tq, tk), lambda h: (h, 0, 0))] * 2,
        out_shape=[jax.ShapeDtypeStruct((H, tq, tk), F32)] * 2,
        name="t5_bias_tiles",
    )(rel_bias)


def _dec_bias_kernel(rb_ref, o_ref, *, page):
    shape = o_ref.shape[1:]
    row_head = lax.broadcasted_iota(jnp.int32, shape, 0) & (H_DIFF - 1)
    lane = lax.broadcasted_iota(jnp.int32, shape, 1)
    own_head = (lane & (H_DIFF - 1)) == row_head
    rel = page - (lane >> (H_DIFF.bit_length() - 1))
    last = jnp.zeros(shape, F32)
    new = jnp.zeros(shape, F32)
    for h in range(H_DIFF):
        sel = row_head == h
        last = jnp.where(sel, _bias_of_rel(rb_ref, h, rel), last)
        new = jnp.where(sel, rb_ref[0, h] - rb_ref[NUM_BUCKETS - 1, h], new)
    o_ref[0] = jnp.where(own_head, 0.0, -jnp.inf)
    o_ref[1] = jnp.where(own_head, last, -jnp.inf)
    o_ref[2] = new


def _dec_bias(rel_bias, page):
    assert page + 1 >= _FAR_REL and H_DIFF & (H_DIFF - 1) == 0
    return pl.pallas_call(
        functools.partial(_dec_bias_kernel, page=page),
        in_specs=[pl.BlockSpec(memory_space=pltpu.SMEM)],
        out_specs=pl.BlockSpec(memory_space=pltpu.VMEM),
        out_shape=jax.ShapeDtypeStruct((3, 2 * H_DIFF, page * H_DIFF), F32),
        name="t5_bias_decode",
    )(rel_bias)


def _unit_lower_inverse(L, ri, ci):
    C = L.shape[-1]
    eye = (ri == ci).astype(F32)
    T = None
    lb = 0
    while (1 << lb) < C:
        below = ((ri >> (lb + 1)) == (ci >> (lb + 1))) & (((ri >> lb) & 1) == 1) & (((ci >> lb) & 1) == 0)
        Lb = jnp.where(below, L, 0.0)
        T = eye - Lb if T is None else T - _bdot(_bdot(T, Lb), T)
        lb += 1
    return T


def _delta_chunk_local(q, k, v, beta, g):
    N, C, _ = q.shape
    ri = lax.broadcasted_iota(jnp.int32, (N, C, C), 1)
    ci = lax.broadcasted_iota(jnp.int32, (N, C, C), 2)
    tril = ri >= ci
    g_cols = jnp.broadcast_to(g, (N, C, C))
    g_row = jnp.sum(jnp.where(ri == ci, g_cols, 0.0), axis=1, keepdims=True)
    cum_col = jnp.sum(jnp.where(tril, jnp.broadcast_to(g_row, (N, C, C)), 0.0), axis=2, keepdims=True)
    cum_row = jnp.sum(jnp.where(ri <= ci, g_cols, 0.0), axis=1, keepdims=True)
    decay = jnp.where(tril, jnp.exp(jnp.where(tril, cum_col - cum_row, 0.0)), 0.0)
    g_last = jnp.sum(g, axis=1, keepdims=True)
    e_col = jnp.exp(cum_col)
    kb = k * beta
    vb = v * beta
    L = jnp.where(ri > ci, _bdot_nt(kb, k) * decay, 0.0)
    T = _unit_lower_inverse(L, ri, ci)
    vk = _bdot(T, jnp.concatenate([vb, kb * e_col], axis=2))
    qk = _bdot_nt(q, k) * decay
    return (vk[:, :, :DN_DV], vk[:, :, DN_DV:], q * e_col, qk, k * jnp.exp(g_last - cum_col),
            jnp.broadcast_to(jnp.exp(g_last), (N, SUBLANES, LANES)))


def _delta_kernel(zq_ref, zk_ref, zv_ref, zg_ref, ba_ref, wq_ref, wk_ref, wv_ref, alog_ref, dtb_ref,
                  gain_ref, o_ref, sout_ref, xq_s, xk_s, xv_s, q_s, k_s, v_s, S_s,
                  val_s, kcum_s, qe_s, qk_s, kdec_s, egl_s, *, HG, TT, C):
    hgi = pl.program_id(1)
    t = pl.program_id(2)
    W = HG * LANES
    NC = TT // C

    @pl.when(t == 0)
    def _():
        for x_s in (xq_s, xk_s, xv_s):
            x_s[0:SUBLANES, :] = jnp.zeros((SUBLANES, W), F32)
        S_s[...] = jnp.zeros_like(S_s)

    for z_ref, x_s, w_ref, dst in ((zq_ref, xq_s, wq_ref, q_s), (zk_ref, xk_s, wk_ref, k_s),
                                   (zv_ref, xv_s, wv_ref, v_s)):
        x_s[SUBLANES:SUBLANES + TT, :] = z_ref[...]
        base = SUBLANES - (CONV_W - 1)
        acc = x_s[base:base + TT, :] * w_ref[0:1, :]
        for j in range(1, CONV_W):
            acc = acc + x_s[base + j:base + j + TT, :] * w_ref[j:j + 1, :]
        dst[...] = _silu(acc)
        x_s[0:SUBLANES, :] = x_s[TT:TT + SUBLANES, :]

    ba = ba_ref[...]
    bt = jax.nn.sigmoid(ba)
    gt = -jnp.exp(alog_ref[...]) * _softplus(ba + dtb_ref[...])
    lane = lax.broadcasted_iota(jnp.int32, (TT, LANES), 1)
    qs, ks, vs, betas, gs = [], [], [], [], []
    for hh in range(HG):
        lanes = slice(hh * LANES, (hh + 1) * LANES)
        head = hgi * HG + hh
        qh = q_s[:, lanes]
        qh = qh * lax.rsqrt(jnp.sum(qh * qh, axis=-1, keepdims=True) + EPS) * (DN_DK ** -0.5)
        kh = k_s[:, lanes]
        kh = kh * lax.rsqrt(jnp.sum(kh * kh, axis=-1, keepdims=True) + EPS)
        qs.append(qh.reshape(NC, C, LANES))
        ks.append(kh.reshape(NC, C, LANES))
        vs.append(v_s[:, lanes].reshape(NC, C, LANES))
        betas.append(jnp.sum(jnp.where(lane == head, bt, 0.0), axis=-1, keepdims=True).reshape(NC, C, 1))
        gs.append(jnp.sum(jnp.where(lane == head + H_DN, gt, 0.0), axis=-1, keepdims=True).reshape(NC, C, 1))

    cat = functools.partial(jnp.concatenate, axis=0)
    local = _delta_chunk_local(cat(qs), cat(ks), cat(vs), cat(betas), cat(gs))
    for ref, val in zip((val_s, kcum_s, qe_s, qk_s, kdec_s, egl_s), local):
        ref[...] = val.reshape(ref.shape)

    for c in range(NC):
        rows = slice(c * C, (c + 1) * C)
        S = S_s[...]
        r = _bdot(jnp.concatenate([kcum_s[:, c], qe_s[:, c]], axis=1), S)
        v_new = val_s[:, c] - r[:, :C]
        o = r[:, C:] + _bdot(qk_s[:, c], v_new)
        S_s[...] = S * egl_s[:, c, 0:1, 0:1] + _bdot_tn(kdec_s[:, c], v_new)
        for hh in range(HG):
            lanes = slice(hh * LANES, (hh + 1) * LANES)
            o_ref[rows, lanes] = (_rms(o[hh], gain_ref[...]) * _silu(zg_ref[rows, lanes])).astype(o_ref.dtype)

    @pl.when(t == pl.num_programs(2) - 1)
    def _():
        sout_ref[...] = S_s[...]


def _delta_prompt(z, ba, conv_w, alog_row, dtb_row, gain_dn, B, T, *, HG=4, TT=512, C=64, name):
    M = z.shape[0]
    TT = min(TT, T)
    C = min(C, TT)
    assert T % TT == 0 and TT % C == 0 and H_DN % HG == 0 and C & (C - 1) == 0 and C % 16 == 0
    NT = T // TT
    NC = TT // C
    W = HG * LANES
    nhg = H_DN // HG

    def zspec(col0):
        return pl.BlockSpec((TT, W), lambda b, hg, t: (b * NT + t, col0 // W + hg))

    def wspec(col0):
        return pl.BlockSpec((CONV_W, W), lambda b, hg, t: (0, col0 // W + hg))

    row = pl.BlockSpec((1, LANES), lambda b, hg, t: (0, 0))
    blocks = [((TT, W), F32)] * 4 + [((TT, LANES), F32)] + [((TT, W), BF16), ((HG, DN_DK, DN_DV), F32)]
    scratch = ([((TT + SUBLANES, W), F32)] * 3 + [((TT, W), F32)] * 3 + [((HG, DN_DK, DN_DV), F32)]
               + [((HG, NC, C, LANES), F32)] * 3 + [((HG, NC, C, C), F32)] + [((HG, NC, C, LANES), F32)]
               + [((HG, NC, SUBLANES, LANES), F32)])
    temporaries = [((HG * NC, C, LANES), F32)] * 12
    return pl.pallas_call(
        functools.partial(_delta_kernel, HG=HG, TT=TT, C=C),
        grid=(B, nhg, NT),
        in_specs=[zspec(_C_QDN), zspec(_C_KDN), zspec(_C_VDN), zspec(_C_GATE),
                  pl.BlockSpec((TT, LANES), lambda b, hg, t: (b * NT + t, 0)),
                  wspec(0), wspec(H_DN * DN_DK), wspec(2 * H_DN * DN_DK), row, row, row],
        out_specs=[pl.BlockSpec((TT, W), lambda b, hg, t: (b * NT + t, hg)),
                   pl.BlockSpec((None, HG, DN_DK, DN_DV), lambda b, hg, t: (b, hg, 0, 0))],
        out_shape=[jax.ShapeDtypeStruct((M, H_DN * DN_DV), BF16),
                   jax.ShapeDtypeStruct((B, H_DN, DN_DK, DN_DV), F32)],
        scratch_shapes=[pltpu.VMEM(shape, dtype) for shape, dtype in scratch],
        compiler_params=pltpu.CompilerParams(
            dimension_semantics=("parallel", "parallel", "arbitrary"),
            vmem_limit_bytes=_vmem_limit(blocks, scratch + temporaries)),
        name=name,
    )(z, z, z, z, ba, conv_w, conv_w, conv_w, alog_row, dtb_row, gain_dn.reshape(1, LANES))


def _lambda(lam_ref, lam_init):
    lv = lam_ref[...]
    s1 = jnp.sum(lv[0:1, :] * lv[1:2, :], axis=-1, keepdims=True)
    s2 = jnp.sum(lv[2:3, :] * lv[3:4, :], axis=-1, keepdims=True)
    return jnp.exp(s1) - jnp.exp(s2) + lam_init


def _attn_kernel(zq_ref, zk_ref, zv_ref, diag_ref, sub_ref, lam_ref, gain_ref, o_ref, kb_s, vb_s, *, tq, lam_init):
    T = zk_ref.shape[0]
    kb_s[...] = zk_ref[...].astype(BF16)
    vb_s[...] = zv_ref[...].astype(BF16)
    lam = _lambda(lam_ref, lam_init)
    lane = lax.broadcasted_iota(jnp.int32, (tq, LANES), 1)
    for qi in range(T // tq):
        rows = slice(qi * tq, (qi + 1) * tq)
        n_keys = (qi + 1) * tq
        q = zq_ref[rows, :] * (DIFF_DQ ** -0.5)
        ks = kb_s[0:n_keys, :]
        vs = vb_s[0:n_keys, :]
        outs = []
        for mi in range(2):
            in_map = (lane >= DIFF_DQ) if mi else (lane < DIFF_DQ)
            qm = jnp.where(in_map, q, 0.0).astype(BF16)
            s = lax.dot_general(qm, ks, (((1,), (1,)), ((), ())), preferred_element_type=F32)
            parts = []
            if qi >= 2:
                parts.append(s[:, :(qi - 1) * tq])
            if qi >= 1:
                parts.append(s[:, (qi - 1) * tq:qi * tq] + sub_ref[...])
            parts.append(s[:, qi * tq:] + diag_ref[...])
            s = parts[0] if len(parts) == 1 else jnp.concatenate(parts, axis=1)
            p = jnp.exp(s - jnp.max(s, axis=-1, keepdims=True))
            l = jnp.sum(p, axis=-1, keepdims=True)
            outs.append(jnp.dot(p.astype(BF16), vs, preferred_element_type=F32) / l)
        o = outs[0] - lam * outs[1]
        o_ref[rows, :] = (_rms(o, gain_ref[...]) * (1.0 - lam_init)).astype(o_ref.dtype)


def _attn_prompt(z, diag, sub, lam_rows, gain_df, B, T, lam_init, *, tq, name):
    M = z.shape[0]
    assert T % tq == 0
    row = pl.BlockSpec((1, LANES), lambda b, h: (0, 0))
    blocks = [((T, LANES), F32)] * 3 + [((tq, tq), F32)] * 2 + [((T, LANES), BF16)]
    resident = [((T, LANES), BF16)] * 2 + [((tq, T), F32)] * 4
    return pl.pallas_call(
        functools.partial(_attn_kernel, tq=tq, lam_init=lam_init),
        grid=(B, H_DIFF),
        in_specs=[pl.BlockSpec((T, LANES), lambda b, h: (b, _C_QDF // LANES + h)),
                  pl.BlockSpec((T, LANES), lambda b, h: (b, _C_KDF // LANES + h)),
                  pl.BlockSpec((T, LANES), lambda b, h: (b, _C_VDF // LANES + h)),
                  pl.BlockSpec((None, tq, tq), lambda b, h: (h, 0, 0)),
                  pl.BlockSpec((None, tq, tq), lambda b, h: (h, 0, 0)),
                  pl.BlockSpec((SUBLANES, LANES), lambda b, h: (0, 0)), row],
        out_specs=pl.BlockSpec((T, LANES), lambda b, h: (b, h)),
        out_shape=jax.ShapeDtypeStruct((M, H_DIFF * DIFF_DV), BF16),
        scratch_shapes=[pltpu.VMEM((T, LANES), BF16)] * 2,
        compiler_params=pltpu.CompilerParams(
            dimension_semantics=("parallel", "parallel"),
            vmem_limit_bytes=_vmem_limit(blocks, resident)),
        name=name,
    )(z, z, z, diag, sub, lam_rows, gain_df.reshape(1, LANES))


def _delta_step_kernel(zq_ref, zk_ref, zv_ref, zg_ref, ba_ref, cq_ref, ck_ref, cv_ref, wq_ref, wk_ref, wv_ref,
                       alog_ref, dtb_ref, gain_ref, s_ref, o_ref, sout_ref):
    def conv(z_ref, c_ref, w_ref):
        acc = z_ref[...] * w_ref[CONV_W - 1:CONV_W, :]
        for j in range(CONV_W - 1):
            acc = acc + c_ref[j:j + 1, :] * w_ref[j:j + 1, :]
        return _silu(acc)

    cq, ck, cv = conv(zq_ref, cq_ref, wq_ref), conv(zk_ref, ck_ref, wk_ref), conv(zv_ref, cv_ref, wv_ref)
    ba = ba_ref[...]
    bt = jax.nn.sigmoid(ba)
    gt = -jnp.exp(alog_ref[...]) * _softplus(ba + dtb_ref[...])
    lane = lax.broadcasted_iota(jnp.int32, (1, LANES), 1)
    pad = jnp.zeros((SUBLANES - 2, LANES), F32)
    for h in range(H_DN):
        lanes = slice(h * LANES, (h + 1) * LANES)
        q = cq[:, lanes]
        q = q * lax.rsqrt(jnp.sum(q * q, axis=-1, keepdims=True) + EPS) * (DN_DK ** -0.5)
        k = ck[:, lanes]
        k = k * lax.rsqrt(jnp.sum(k * k, axis=-1, keepdims=True) + EPS)
        v = cv[:, lanes]
        beta = jnp.sum(jnp.where(lane == h, bt, 0.0), axis=-1, keepdims=True)
        eg = jnp.exp(jnp.sum(jnp.where(lane == h + H_DN, gt, 0.0), axis=-1, keepdims=True))
        S = s_ref[h]
        r = _dot(jnp.concatenate([k, q, pad], axis=0), S)
        v_new = beta * (v - eg * r[0:1])
        o = eg * r[1:2] + jnp.sum(q * k, axis=-1, keepdims=True) * v_new
        zeros7 = jnp.zeros((SUBLANES - 1, LANES), F32)
        outer = _dot_tn(jnp.concatenate([k, zeros7], axis=0), jnp.concatenate([v_new, zeros7], axis=0))
        sout_ref[h] = S * eg + outer
        o_ref[:, lanes] = (_rms(o, gain_ref[...]) * _silu(zg_ref[:, lanes])).astype(o_ref.dtype)


def _delta_step(z3, ba3, conv_state, conv_w, alog_row, dtb_row, gain_dn, state, layer, *, name):
    DB = z3.shape[0]
    W = H_DN * LANES

    def zspec(col0):
        return pl.BlockSpec((None, 1, W), lambda b: (b, 0, col0 // W))

    def cspec(col0):
        return pl.BlockSpec((None, None, CONV_W - 1, W), lambda b: (layer, b, 0, col0 // W))

    def wspec(col0):
        return pl.BlockSpec((CONV_W, W), lambda b: (0, col0 // W))

    row = pl.BlockSpec((1, LANES), lambda b: (0, 0))
    return pl.pallas_call(
        _delta_step_kernel,
        grid=(DB,),
        in_specs=[zspec(_C_QDN), zspec(_C_KDN), zspec(_C_VDN), zspec(_C_GATE),
                  pl.BlockSpec((None, 1, LANES), lambda b: (b, 0, 0)),
                  cspec(0), cspec(W), cspec(2 * W), wspec(0), wspec(W), wspec(2 * W), row, row, row,
                  pl.BlockSpec((None, None, H_DN, DN_DK, DN_DV), lambda b: (layer, b, 0, 0, 0))],
        out_specs=[pl.BlockSpec((None, 1, W), lambda b: (b, 0, 0)),
                   pl.BlockSpec((None, H_DN, DN_DK, DN_DV), lambda b: (b, 0, 0, 0))],
        out_shape=[jax.ShapeDtypeStruct((DB, 1, W), F32),
                   jax.ShapeDtypeStruct((DB, H_DN, DN_DK, DN_DV), F32)],
        compiler_params=pltpu.CompilerParams(dimension_semantics=("parallel",)),
        name=name,
    )(z3, z3, z3, z3, ba3, conv_state, conv_state, conv_state, conv_w, conv_w, conv_w,
      alog_row, dtb_row, gain_dn.reshape(1, LANES), state)


def _decode_kernel(pt_ref, qt_ref, knew_ref, vnew_ref, bias_ref, lam_ref, gain_ref, *rest, G, lam_init):
    k_refs = rest[:G]
    v_refs = rest[G:2 * G]
    o_ref = rest[2 * G]
    m_s, l_s, acc_s = rest[2 * G + 1:]
    step = pl.program_id(1)
    nsteps = pl.num_programs(1)

    @pl.when(step == 0)
    def _():
        m_s[...] = jnp.full(m_s.shape, -jnp.inf, F32)
        l_s[...] = jnp.zeros_like(l_s)
        acc_s[...] = jnp.zeros_like(acc_s)

    qt = qt_ref[...]
    scores = []
    for i in range(G):
        s = lax.dot_general(qt, k_refs[i][...].astype(BF16), (((1,), (1,)), ((), ())),
                            preferred_element_type=F32)
        if i == G - 1:
            s = s + jnp.where(step == nsteps - 1, bias_ref[1], bias_ref[0])
        else:
            s = s + bias_ref[0]
        scores.append(s)
    m_old = m_s[...]
    m_new = m_old
    for s in scores:
        m_new = jnp.maximum(m_new, jnp.max(s, axis=-1, keepdims=True))
    alpha = jnp.exp(m_old - m_new)
    l_new = alpha * l_s[...]
    acc = alpha * acc_s[...]
    for i in range(G):
        p = jnp.exp(scores[i] - m_new)
        l_new = l_new + jnp.sum(p, axis=-1, keepdims=True)
        acc = acc + jnp.dot(p.astype(BF16), v_refs[i][...].astype(BF16), preferred_element_type=F32)
    m_s[...] = m_new
    l_s[...] = l_new
    acc_s[...] = acc

    @pl.when(step == nsteps - 1)
    def _():
        kn = knew_ref[...].astype(BF16).astype(F32)
        kn = jnp.concatenate([kn, kn], axis=0)
        vn = jnp.concatenate([vnew_ref[...], vnew_ref[...]], axis=0)
        s = jnp.sum(qt.astype(F32) * kn, axis=-1, keepdims=True) + bias_ref[2][:, 0:1]
        m_o = m_s[...]
        m_n = jnp.maximum(m_o, s)
        a = jnp.exp(m_o - m_n)
        p = jnp.exp(s - m_n)
        l_f = a * l_s[...] + p
        acc_f = (a * acc_s[...] + p * vn) / l_f
        o = acc_f[0:H_DIFF] - _lambda(lam_ref, lam_init) * acc_f[H_DIFF:2 * H_DIFF]
        o_ref[...] = _rms(o, gain_ref[...]) * (1.0 - lam_init)


def _decode_attn(qt, knew, vnew, dec_bias, lam_rows, gain_df, pages_k, pages_v, page_table, layer,
                 lam_init, *, G=4, name):
    DB, n_pages = page_table.shape
    PH = pages_k.shape[2]
    R = 2 * H_DIFF
    G = min(G, n_pages)
    assert n_pages % G == 0

    def pspec(i):
        return pl.BlockSpec((None, None, PH, LANES), lambda b, s, pt: (layer, pt[b, s * G + i], 0, 0))

    def per_b(rows):
        return pl.BlockSpec((None, rows, LANES), lambda b, s, pt: (b, 0, 0))

    row = pl.BlockSpec((1, LANES), lambda b, s, pt: (0, 0))
    blocks = [((PH, LANES), F32)] * (2 * G) + [((3, R, PH), F32)]
    grid_spec = pltpu.PrefetchScalarGridSpec(
        num_scalar_prefetch=1,
        grid=(DB, n_pages // G),
        in_specs=[per_b(R), per_b(H_DIFF), per_b(H_DIFF),
                  pl.BlockSpec((3, R, PH), lambda b, s, pt: (0, 0, 0)),
                  pl.BlockSpec((SUBLANES, LANES), lambda b, s, pt: (0, 0)), row]
        + [pspec(i) for i in range(G)] + [pspec(i) for i in range(G)],
        out_specs=per_b(H_DIFF),
        scratch_shapes=[pltpu.VMEM((R, 1), F32), pltpu.VMEM((R, 1), F32), pltpu.VMEM((R, LANES), F32)],
    )
    return pl.pallas_call(
        functools.partial(_decode_kernel, G=G, lam_init=lam_init),
        grid_spec=grid_spec,
        out_shape=jax.ShapeDtypeStruct((DB, H_DIFF, LANES), F32),
        compiler_params=pltpu.CompilerParams(
            dimension_semantics=("parallel", "arbitrary"), vmem_limit_bytes=_vmem_limit(blocks)),
        name=name,
    )(page_table, qt, knew, vnew, dec_bias, lam_rows, gain_df.reshape(1, LANES),
      *([pages_k] * G), *([pages_v] * G))


def _pad_rows(x, rows):
    return jnp.pad(x, ((0, rows - x.shape[0]), (0, 0)))


def kernel(x_prompt, x_sample, cache_k, cache_v, state_delta, state_conv, page_table, rel_bias, norm_mix, w_in, conv_w, a_log, dt_bias, norm_dn, w_o_dn, lam_q1, lam_k1, lam_q2, lam_k2, norm_diff, w_o_diff, w_out, norm_mlp, w_up, w_down, norm_final):
    B, T, D = x_prompt.shape
    DB = x_sample.shape[0]
    depth = w_in.shape[0]
    n_pool, page = cache_k.shape[1], cache_k.shape[2]
    conv_dim = conv_w.shape[-1]
    n_ba = 2 * H_DN
    assert conv_dim == 3 * H_DN * DN_DK and x_sample.shape[1] == 1
    MS = 16
    tq = min(256, T)

    xp = x_prompt.reshape(B * T, D)
    xs = _pad_rows(x_sample.reshape(DB, D), MS)
    diag, sub = _bias_tiles(rel_bias, tq, tq)
    dec_bias = _dec_bias(rel_bias, page)
    pages_k = cache_k.reshape(depth, n_pool, page * H_DIFF, 2 * DIFF_DQ)
    pages_v = cache_v.reshape(depth, n_pool, page * H_DIFF, DIFF_DV)
    in_map0 = jnp.arange(2 * DIFF_DQ) < DIFF_DQ

    kp_l, vp_l, sp_l, cp_l, ks_l, vs_l, ss_l, cs_l = [], [], [], [], [], [], [], []
    for l in range(depth):
        lam_init = 0.8 - 0.6 * math.exp(-0.3 * l)
        wl = w_in[l]
        w_main = jnp.concatenate([wl[:, :conv_dim], wl[:, conv_dim + n_ba:]], axis=1).astype(BF16)
        w_ba = jnp.pad(wl[:, conv_dim:conv_dim + n_ba], ((0, 0), (0, LANES - n_ba))).astype(BF16)
        w_odn, w_odf, w_o = w_o_dn[l].astype(BF16), w_o_diff[l].astype(BF16), w_out[l].astype(BF16)
        w_u, w_d = w_up[l].astype(BF16), w_down[l].astype(BF16)
        alog_row = jnp.pad(a_log[l], (H_DN, LANES - n_ba)).reshape(1, LANES)
        dtb_row = jnp.pad(dt_bias[l], (H_DN, LANES - n_ba)).reshape(1, LANES)
        lam_rows = jnp.pad(jnp.stack([lam_q1[l], lam_k1[l], lam_q2[l], lam_k2[l]]),
                           ((0, SUBLANES - 4), (0, LANES - DIFF_DQ)))
        last = l == depth - 1

        z = _mm(xp, w_main, gain=norm_mix[l], name=f"in_proj_p{l}")
        ba = _mm(xp, w_ba, gain=norm_mix[l], name=f"in_proj_ba_p{l}")
        o_dn, s_fin = _delta_prompt(z, ba, conv_w[l], alog_row, dtb_row, norm_dn[l], B, T, name=f"delta_p{l}")
        o_df = _attn_prompt(z, diag, sub, lam_rows, norm_diff[l], B, T, lam_init, tq=tq, name=f"attn_p{l}")
        merged = _merge(o_dn, o_df, z, w_odn, w_odf, name=f"merge_p{l}")
        xp = _mm(merged, w_o, residual=xp, name=f"out_proj_p{l}")
        xp = _mlp(xp, norm_mlp[l], w_u, w_d, final_gain=norm_final if last else None, name=f"mlp_p{l}")
        z3 = z.reshape(B, T, -1)
        kp_l.append(z3[:, :, _C_KDF:_C_KDF + 1024].reshape(B, T, H_DIFF, 2 * DIFF_DQ))
        vp_l.append(z3[:, :, _C_VDF:_C_VDF + 1024].reshape(B, T, H_DIFF, DIFF_DV))
        sp_l.append(s_fin)
        cp_l.append(z3[:, T - (CONV_W - 1):, :conv_dim])

        zs = _mm(xs, w_main, gain=norm_mix[l], tn=1024, name=f"in_proj_s{l}")
        bas = _mm(xs, w_ba, gain=norm_mix[l], name=f"in_proj_ba_s{l}")
        zs3 = zs[:DB].reshape(DB, 1, -1)
        o_dn_s, s_new = _delta_step(zs3, bas[:DB].reshape(DB, 1, LANES), state_conv, conv_w[l], alog_row,
                                    dtb_row, norm_dn[l], state_delta, l, name=f"delta_s{l}")
        q_s = zs[:DB, _C_QDF:_C_QDF + 1024].reshape(DB, H_DIFF, 2 * DIFF_DQ) * (DIFF_DQ ** -0.5)
        qt = jnp.concatenate([jnp.where(in_map0, q_s, 0.0), jnp.where(in_map0, 0.0, q_s)], axis=1).astype(BF16)
        k_s = zs3[:, :, _C_KDF:_C_KDF + 1024]
        v_s = zs3[:, :, _C_VDF:_C_VDF + 1024]
        o_df_s = _decode_attn(qt, k_s.reshape(DB, H_DIFF, 2 * DIFF_DQ), v_s.reshape(DB, H_DIFF, DIFF_DV),
                              dec_bias, lam_rows, norm_diff[l], pages_k, pages_v, page_table, l,
                              lam_init, name=f"attn_s{l}")
        merged_s = _merge(_pad_rows(o_dn_s.reshape(DB, -1), MS).astype(BF16),
                          _pad_rows(o_df_s.reshape(DB, -1), MS).astype(BF16), zs,
                          w_odn, w_odf, tn=1024, name=f"merge_s{l}")
        xs = _mm(merged_s, w_o, residual=xs, tn=1024, name=f"out_proj_s{l}")
        xs = _mlp(xs, norm_mlp[l], w_u, w_d, final_gain=norm_final if last else None, name=f"mlp_s{l}")
        ks_l.append(k_s.reshape(DB, 1, H_DIFF, 2 * DIFF_DQ))
        vs_l.append(v_s.reshape(DB, 1, H_DIFF, DIFF_DV))
        ss_l.append(s_new)
        cs_l.append(jnp.concatenate([state_conv[l][:, 1:], zs3[:, :, :conv_dim]], axis=1))

    y_prompt = xp.reshape(B, T, D)
    y_sample = xs[:DB].reshape(DB, 1, D)
    return (y_prompt, y_sample,
            jnp.stack(kp_l), jnp.stack(vp_l), jnp.stack(sp_l), jnp.stack(cp_l),
            jnp.stack(ks_l), jnp.stack(vs_l), jnp.stack(ss_l), jnp.stack(cs_l))
```

```python
import functools
import math

import jax
import jax.numpy as jnp
from jax import lax
from jax.experimental import pallas as pl
from jax.experimental.pallas import tpu as pltpu

F32 = jnp.float32
BF16 = jnp.bfloat16

H_DN = 8
DN_DK = 128
DN_DV = 128
CONV_W = 4
H_DIFF = 8
DIFF_DQ = 64
DIFF_DV = 128
NUM_BUCKETS = 32
MAX_DISTANCE = 128
EPS = 1e-6

LANES = 128
SUBLANES = 8
VMEM_BYTES_V7X = 64 * 1024 * 1024
VMEM_INTERNAL_ALLOWANCE = 12 * 1024 * 1024

_MAX_EXACT = NUM_BUCKETS // 2
_BUCKET_START = tuple(
    k if k <= _MAX_EXACT else math.ceil(
        _MAX_EXACT * (MAX_DISTANCE / _MAX_EXACT) ** ((k - _MAX_EXACT) / (NUM_BUCKETS - _MAX_EXACT)))
    for k in range(NUM_BUCKETS))
_FAR_REL = _BUCKET_START[-1]

_C_QDN, _C_KDN, _C_VDN, _C_GATE = 0, 1024, 2048, 3072
_C_QDF = 4096
_C_GA = 5120


def _nbytes(shape, dtype):
    return math.prod(shape) * jnp.dtype(dtype).itemsize


def _vmem_limit(pipelined, resident=()):
    need = 2 * sum(_nbytes(s, d) for s, d in pipelined) + sum(_nbytes(s, d) for s, d in resident)
    return int(min(need + VMEM_INTERNAL_ALLOWANCE, VMEM_BYTES_V7X - 4 * 1024 * 1024))


def _dot(a, b):
    return jnp.dot(a.astype(BF16), b.astype(BF16), preferred_element_type=F32)


def _dot_nt(a, b):
    return lax.dot_general(a.astype(BF16), b.astype(BF16), (((1,), (1,)), ((), ())),
                           preferred_element_type=F32)


def _dot_tn(a, b):
    return lax.dot_general(a.astype(BF16), b.astype(BF16), (((0,), (0,)), ((), ())),
                           preferred_element_type=F32)


def _bdot(a, b):
    return lax.dot_general(a.astype(BF16), b.astype(BF16), (((2,), (1,)), ((0,), (0,))),
                           preferred_element_type=F32)


def _bdot_nt(a, b):
    return lax.dot_general(a.astype(BF16), b.astype(BF16), (((2,), (2,)), ((0,), (0,))),
                           preferred_element_type=F32)


def _bdot_tn(a, b):
    return lax.dot_general(a.astype(BF16), b.astype(BF16), (((1,), (1,)), ((0,), (0,))),
                           preferred_element_type=F32)


def _rms(x, gain):
    return x * lax.rsqrt(jnp.mean(x * x, axis=-1, keepdims=True) + EPS) * gain


def _silu(x):
    h = 0.5 * x
    return h + h * jnp.tanh(h)


def _softplus(x):
    return jnp.maximum(x, 0.0) + jnp.log1p(jnp.exp(-jnp.abs(x)))


def _norm_rows_to(x_ref, g_ref, h_ref):
    tm = x_ref.shape[0]
    rows = min(tm, 128)

    def body(i, c):
        sl = pl.ds(pl.multiple_of(i * rows, rows), rows)
        h_ref[sl, :] = _rms(x_ref[sl, :], g_ref[...]).astype(h_ref.dtype)
        return c

    lax.fori_loop(0, tm // rows, body, 0)


def _mm_kernel(*refs, has_gain, has_res):
    it = iter(refs)
    x_ref = next(it)
    g_ref = next(it) if has_gain else None
    w_ref = next(it)
    r_ref = next(it) if has_res else None
    o_ref = next(it)
    if has_gain:
        h_ref = next(it)

        @pl.when(pl.program_id(1) == 0)
        def _():
            _norm_rows_to(x_ref, g_ref, h_ref)

        lhs = h_ref[...]
    else:
        lhs = x_ref[...]
    acc = jnp.dot(lhs, w_ref[...], preferred_element_type=F32)
    if has_res:
        acc = acc + r_ref[...]
    o_ref[...] = acc.astype(o_ref.dtype)


def _mm(x, w, *, gain=None, residual=None, out_dtype=F32, tm=1024, tn=512, name):
    M, K = x.shape
    N = w.shape[1]
    tm, tn = min(tm, M), min(tn, N)
    assert M % tm == 0 and N % tn == 0, (M, N, tm, tn)
    in_specs = [pl.BlockSpec((tm, K), lambda i, j: (i, 0))]
    args = [x]
    blocks = [((tm, K), x.dtype), ((K, tn), w.dtype), ((tm, tn), out_dtype)]
    scratch = []
    if gain is not None:
        in_specs.append(pl.BlockSpec((1, K), lambda i, j: (0, 0)))
        args.append(gain.reshape(1, K))
        scratch.append(pltpu.VMEM((tm, K), BF16))
    in_specs.append(pl.BlockSpec((K, tn), lambda i, j: (0, j)))
    args.append(w)
    if residual is not None:
        in_specs.append(pl.BlockSpec((tm, tn), lambda i, j: (i, j)))
        args.append(residual)
        blocks.append(((tm, tn), residual.dtype))
    return pl.pallas_call(
        functools.partial(_mm_kernel, has_gain=gain is not None, has_res=residual is not None),
        grid=(M // tm, N // tn),
        in_specs=in_specs,
        out_specs=pl.BlockSpec((tm, tn), lambda i, j: (i, j)),
        out_shape=jax.ShapeDtypeStruct((M, N), out_dtype),
        scratch_shapes=scratch,
        compiler_params=pltpu.CompilerParams(
            dimension_semantics=("parallel", "arbitrary"),
            vmem_limit_bytes=_vmem_limit(blocks, [((tm, K), BF16)] if gain is not None else [])),
        name=name,
    )(*args)


def _mm_split_kernel(x_ref, g_ref, w_ref, *rest, widths):
    out_refs, h_ref = rest[:-1], rest[-1]
    _norm_rows_to(x_ref, g_ref, h_ref)
    acc = jnp.dot(h_ref[...], w_ref[...], preferred_element_type=F32)
    col = 0
    for o_ref, width in zip(out_refs, widths):
        o_ref[...] = acc[:, col:col + width]
        col += width


def _mm_split(x, w, gain, widths, *, tm=512, name):
    M, K = x.shape
    N = w.shape[1]
    tm = min(tm, M)
    assert M % tm == 0 and sum(widths) == N and all(wd % LANES == 0 for wd in widths)
    blocks = [((tm, K), x.dtype), ((K, N), w.dtype), ((tm, N), F32)]
    return pl.pallas_call(
        functools.partial(_mm_split_kernel, widths=tuple(widths)),
        grid=(M // tm,),
        in_specs=[pl.BlockSpec((tm, K), lambda i: (i, 0)), pl.BlockSpec((1, K), lambda i: (0, 0)),
                  pl.BlockSpec((K, N), lambda i: (0, 0))],
        out_specs=[pl.BlockSpec((tm, wd), lambda i: (i, 0)) for wd in widths],
        out_shape=[jax.ShapeDtypeStruct((M, wd), F32) for wd in widths],
        scratch_shapes=[pltpu.VMEM((tm, K), BF16)],
        compiler_params=pltpu.CompilerParams(
            dimension_semantics=("parallel",),
            vmem_limit_bytes=_vmem_limit(blocks, [((tm, K), BF16), ((tm, N), F32)])),
        name=name,
    )(x, gain.reshape(1, K), w)


def _merge_kernel(odn_ref, odf_ref, ga_ref, gb_ref, wdn_ref, wdf_ref, o_ref):
    a = jnp.dot(odn_ref[...], wdn_ref[...], preferred_element_type=F32)
    b = jnp.dot(odf_ref[...], wdf_ref[...], preferred_element_type=F32)
    ga = jax.nn.sigmoid(ga_ref[...].astype(F32))
    gb = jax.nn.sigmoid(gb_ref[...].astype(F32))
    o_ref[...] = (ga * a + gb * b).astype(o_ref.dtype)


def _merge(o_dn, o_df, z, w_dn, w_df, *, tm=1024, tn=512, name):
    M, K = o_dn.shape
    N = w_dn.shape[1]
    tm, tn = min(tm, M), min(tn, N)
    assert M % tm == 0 and N % tn == 0 and _C_GA % tn == 0
    ga0 = _C_GA // tn
    gb0 = (_C_GA + N) // tn
    blocks = [((tm, K), BF16)] * 2 + [((tm, tn), z.dtype)] * 2 + [((K, tn), BF16)] * 2 + [((tm, tn), BF16)]
    return pl.pallas_call(
        _merge_kernel,
        grid=(M // tm, N // tn),
        in_specs=[
            pl.BlockSpec((tm, K), lambda i, j: (i, 0)),
            pl.BlockSpec((tm, K), lambda i, j: (i, 0)),
            pl.BlockSpec((tm, tn), lambda i, j: (i, ga0 + j)),
            pl.BlockSpec((tm, tn), lambda i, j: (i, gb0 + j)),
            pl.BlockSpec((K, tn), lambda i, j: (0, j)),
            pl.BlockSpec((K, tn), lambda i, j: (0, j)),
        ],
        out_specs=pl.BlockSpec((tm, tn), lambda i, j: (i, j)),
        out_shape=jax.ShapeDtypeStruct((M, N), BF16),
        compiler_params=pltpu.CompilerParams(
            dimension_semantics=("parallel", "arbitrary"), vmem_limit_bytes=_vmem_limit(blocks)),
        name=name,
    )(o_dn, o_df, z, z, w_dn, w_df)


def _mlp_kernel(*refs, has_final):
    if has_final:
        x_ref, g_ref, wu_ref, wd_ref, gf_ref, o_ref, h_ref, acc_ref = refs
    else:
        x_ref, g_ref, wu_ref, wd_ref, o_ref, h_ref, acc_ref = refs
        gf_ref = None
    f = pl.program_id(1)

    @pl.when(f == 0)
    def _():
        _norm_rows_to(x_ref, g_ref, h_ref)
        acc_ref[...] = jnp.zeros_like(acc_ref)

    u = jnp.dot(h_ref[...], wu_ref[...], preferred_element_type=F32)
    u = jnp.square(jnp.maximum(u, 0.0)).astype(BF16)
    acc_ref[...] += jnp.dot(u, wd_ref[...], preferred_element_type=F32)

    @pl.when(f == pl.num_programs(1) - 1)
    def _():
        y = x_ref[...] + acc_ref[...]
        if has_final:
            y = _rms(y, gf_ref[...])
        o_ref[...] = y


def _mlp(x, gain, w_up, w_down, *, final_gain=None, tm=512, tf=1024, name):
    M, D = x.shape
    FF = w_up.shape[1]
    tm, tf = min(tm, M), min(tf, FF)
    assert M % tm == 0 and FF % tf == 0
    in_specs = [
        pl.BlockSpec((tm, D), lambda i, f: (i, 0)),
        pl.BlockSpec((1, D), lambda i, f: (0, 0)),
        pl.BlockSpec((D, tf), lambda i, f: (0, f)),
        pl.BlockSpec((tf, D), lambda i, f: (f, 0)),
    ]
    args = [x, gain.reshape(1, D), w_up, w_down]
    if final_gain is not None:
        in_specs.append(pl.BlockSpec((1, D), lambda i, f: (0, 0)))
        args.append(final_gain.reshape(1, D))
    blocks = [((tm, D), F32), ((D, tf), BF16), ((tf, D), BF16), ((tm, D), F32)]
    resident = [((tm, D), BF16), ((tm, D), F32)]
    return pl.pallas_call(
        functools.partial(_mlp_kernel, has_final=final_gain is not None),
        grid=(M // tm, FF // tf),
        in_specs=in_specs,
        out_specs=pl.BlockSpec((tm, D), lambda i, f: (i, 0)),
        out_shape=jax.ShapeDtypeStruct((M, D), F32),
        scratch_shapes=[pltpu.VMEM((tm, D), BF16), pltpu.VMEM((tm, D), F32)],
        compiler_params=pltpu.CompilerParams(
            dimension_semantics=("parallel", "arbitrary"), vmem_limit_bytes=_vmem_limit(blocks, resident)),
        name=name,
    )(*args)


def _bias_of_rel(rb_ref, h, rel):
    far = rb_ref[NUM_BUCKETS - 1, h]
    v = jnp.full(rel.shape, rb_ref[0, h] - far, F32)
    for k in range(1, NUM_BUCKETS):
        v = jnp.where(rel >= _BUCKET_START[k], rb_ref[k, h] - far, v)
    return v


def _bias_tiles_kernel(rb_ref, diag_ref, sub_ref, *, tk):
    h = pl.program_id(0)
    shape = diag_ref.shape
    rel = lax.broadcasted_iota(jnp.int32, shape, 0) - lax.broadcasted_iota(jnp.int32, shape, 1)
    diag_ref[...] = jnp.where(rel >= 0, _bias_of_rel(rb_ref, h, rel), -jnp.inf)
    sub_ref[...] = _bias_of_rel(rb_ref, h, rel + tk)


def _bias_tiles(rel_bias, tq, tk):
    assert tq == tk and tk + 1 >= _FAR_REL
    H = rel_bias.shape[1]
    return pl.pallas_call(
        functools.partial(_bias_tiles_kernel, tk=tk),
        grid=(H,),
        in_specs=[pl.BlockSpec(memory_space=pltpu.SMEM)],
        out_specs=[pl.BlockSpec((None, tq, tk), lambda h: (h, 0, 0))] * 2,
        out_shape=[jax.ShapeDtypeStruct((H, tq, tk), F32)] * 2,
        name="t5_bias_tiles",
    )(rel_bias)


def _dec_bias_kernel(rb_ref, o_ref, *, page):
    shape = o_ref.shape[1:]
    row_head = lax.broadcasted_iota(jnp.int32, shape, 0) & (H_DIFF - 1)
    lane = lax.broadcasted_iota(jnp.int32, shape, 1)
    own_head = (lane & (H_DIFF - 1)) == row_head
    rel = page - (lane >> (H_DIFF.bit_length() - 1))
    last = jnp.zeros(shape, F32)
    new = jnp.zeros(shape, F32)
    for h in range(H_DIFF):
        sel = row_head == h
        last = jnp.where(sel, _bias_of_rel(rb_ref, h, rel), last)
        new = jnp.where(sel, rb_ref[0, h] - rb_ref[NUM_BUCKETS - 1, h], new)
    o_ref[0] = jnp.where(own_head, 0.0, -jnp.inf)
    o_ref[1] = jnp.where(own_head, last, -jnp.inf)
    o_ref[2] = new


def _dec_bias(rel_bias, page):
    assert page + 1 >= _FAR_REL and H_DIFF & (H_DIFF - 1) == 0
    return pl.pallas_call(
        functools.partial(_dec_bias_kernel, page=page),
        in_specs=[pl.BlockSpec(memory_space=pltpu.SMEM)],
        out_specs=pl.BlockSpec(memory_space=pltpu.VMEM),
        out_shape=jax.ShapeDtypeStruct((3, 2 * H_DIFF, page * H_DIFF), F32),
        name="t5_bias_decode",
    )(rel_bias)


def _unit_lower_inverse(L, ri, ci):
    C = L.shape[-1]
    eye = (ri == ci).astype(F32)
    T = None
    lb = 0
    while (1 << lb) < C:
        below = ((ri >> (lb + 1)) == (ci >> (lb + 1))) & (((ri >> lb) & 1) == 1) & (((ci >> lb) & 1) == 0)
        Lb = jnp.where(below, L, 0.0)
        T = eye - Lb if T is None else T - _bdot(_bdot(T, Lb), T)
        lb += 1
    return T


def _delta_chunk_local(q, k, v, beta, g):
    N, C, _ = q.shape
    ri = lax.broadcasted_iota(jnp.int32, (N, C, C), 1)
    ci = lax.broadcasted_iota(jnp.int32, (N, C, C), 2)
    tril = ri >= ci
    g_cols = jnp.broadcast_to(g, (N, C, C))
    g_row = jnp.sum(jnp.where(ri == ci, g_cols, 0.0), axis=1, keepdims=True)
    cum_col = jnp.sum(jnp.where(tril, jnp.broadcast_to(g_row, (N, C, C)), 0.0), axis=2, keepdims=True)
    cum_row = jnp.sum(jnp.where(ri <= ci, g_cols, 0.0), axis=1, keepdims=True)
    decay = jnp.where(tril, jnp.exp(jnp.where(tril, cum_col - cum_row, 0.0)), 0.0)
    g_last = jnp.sum(g, axis=1, keepdims=True)
    e_col = jnp.exp(cum_col)
    kb = k * beta
    vb = v * beta
    L = jnp.where(ri > ci, _bdot_nt(kb, k) * decay, 0.0)
    T = _unit_lower_inverse(L, ri, ci)
    vk = _bdot(T, jnp.concatenate([vb, kb * e_col], axis=2))
    qk = _bdot_nt(q, k) * decay
    return (vk[:, :, :DN_DV], vk[:, :, DN_DV:], q * e_col, qk, k * jnp.exp(g_last - cum_col),
            jnp.broadcast_to(jnp.exp(g_last), (N, SUBLANES, LANES)))


def _delta_kernel(zq_ref, zk_ref, zv_ref, zg_ref, ba_ref, wq_ref, wk_ref, wv_ref, alog_ref, dtb_ref,
                  gain_ref, o_ref, sout_ref, xq_s, xk_s, xv_s, q_s, k_s, v_s, S_s,
                  val_s, kcum_s, qe_s, qk_s, kdec_s, egl_s, *, HG, TT, C):
    hgi = pl.program_id(1)
    t = pl.program_id(2)
    W = HG * LANES
    NC = TT // C

    @pl.when(t == 0)
    def _():
        for x_s in (xq_s, xk_s, xv_s):
            x_s[0:SUBLANES, :] = jnp.zeros((SUBLANES, W), F32)
        S_s[...] = jnp.zeros_like(S_s)

    for z_ref, x_s, w_ref, dst in ((zq_ref, xq_s, wq_ref, q_s), (zk_ref, xk_s, wk_ref, k_s),
                                   (zv_ref, xv_s, wv_ref, v_s)):
        x_s[SUBLANES:SUBLANES + TT, :] = z_ref[...].astype(F32)
        base = SUBLANES - (CONV_W - 1)
        acc = x_s[base:base + TT, :] * w_ref[0:1, :]
        for j in range(1, CONV_W):
            acc = acc + x_s[base + j:base + j + TT, :] * w_ref[j:j + 1, :]
        dst[...] = _silu(acc)
        x_s[0:SUBLANES, :] = x_s[TT:TT + SUBLANES, :]

    ba = ba_ref[...]
    bt = jax.nn.sigmoid(ba)
    gt = -jnp.exp(alog_ref[...]) * _softplus(ba + dtb_ref[...])
    lane = lax.broadcasted_iota(jnp.int32, (TT, LANES), 1)
    qs, ks, vs, betas, gs = [], [], [], [], []
    for hh in range(HG):
        lanes = slice(hh * LANES, (hh + 1) * LANES)
        head = hgi * HG + hh
        qh = q_s[:, lanes]
        qh = qh * lax.rsqrt(jnp.sum(qh * qh, axis=-1, keepdims=True) + EPS) * (DN_DK ** -0.5)
        kh = k_s[:, lanes]
        kh = kh * lax.rsqrt(jnp.sum(kh * kh, axis=-1, keepdims=True) + EPS)
        qs.append(qh.reshape(NC, C, LANES))
        ks.append(kh.reshape(NC, C, LANES))
        vs.append(v_s[:, lanes].reshape(NC, C, LANES))
        betas.append(jnp.sum(jnp.where(lane == head, bt, 0.0), axis=-1, keepdims=True).reshape(NC, C, 1))
        gs.append(jnp.sum(jnp.where(lane == head + H_DN, gt, 0.0), axis=-1, keepdims=True).reshape(NC, C, 1))

    cat = functools.partial(jnp.concatenate, axis=0)
    local = _delta_chunk_local(cat(qs), cat(ks), cat(vs), cat(betas), cat(gs))
    for ref, val in zip((val_s, kcum_s, qe_s, qk_s, kdec_s, egl_s), local):
        ref[...] = val.reshape(ref.shape)

    for c in range(NC):
        rows = slice(c * C, (c + 1) * C)
        S = S_s[...]
        r = _bdot(jnp.concatenate([kcum_s[:, c], qe_s[:, c]], axis=1), S)
        v_new = val_s[:, c] - r[:, :C]
        o = r[:, C:] + _bdot(qk_s[:, c], v_new)
        S_s[...] = S * egl_s[:, c, 0:1, 0:1] + _bdot_tn(kdec_s[:, c], v_new)
        for hh in range(HG):
            lanes = slice(hh * LANES, (hh + 1) * LANES)
            gate = _silu(zg_ref[rows, lanes].astype(F32))
            o_ref[rows, lanes] = (_rms(o[hh], gain_ref[...]) * gate).astype(o_ref.dtype)

    @pl.when(t == pl.num_programs(2) - 1)
    def _():
        sout_ref[...] = S_s[...]


def _delta_prompt(zb, ba, conv_w, alog_row, dtb_row, gain_dn, B, T, *, HG=4, TT=512, C=64, name):
    M = zb.shape[0]
    TT = min(TT, T)
    C = min(C, TT)
    assert T % TT == 0 and TT % C == 0 and H_DN % HG == 0 and C & (C - 1) == 0 and C % 16 == 0
    NT = T // TT
    NC = TT // C
    W = HG * LANES
    nhg = H_DN // HG

    def zspec(col0):
        return pl.BlockSpec((TT, W), lambda b, hg, t: (b * NT + t, col0 // W + hg))

    def wspec(col0):
        return pl.BlockSpec((CONV_W, W), lambda b, hg, t: (0, col0 // W + hg))

    row = pl.BlockSpec((1, LANES), lambda b, hg, t: (0, 0))
    blocks = [((TT, W), zb.dtype)] * 4 + [((TT, LANES), F32)] + [((TT, W), BF16), ((HG, DN_DK, DN_DV), F32)]
    scratch = ([((TT + SUBLANES, W), F32)] * 3 + [((TT, W), F32)] * 3 + [((HG, DN_DK, DN_DV), F32)]
               + [((HG, NC, C, LANES), F32)] * 3 + [((HG, NC, C, C), F32)] + [((HG, NC, C, LANES), F32)]
               + [((HG, NC, SUBLANES, LANES), F32)])
    temporaries = [((HG * NC, C, LANES), F32)] * 12
    return pl.pallas_call(
        functools.partial(_delta_kernel, HG=HG, TT=TT, C=C),
        grid=(B, nhg, NT),
        in_specs=[zspec(_C_QDN), zspec(_C_KDN), zspec(_C_VDN), zspec(_C_GATE),
                  pl.BlockSpec((TT, LANES), lambda b, hg, t: (b * NT + t, 0)),
                  wspec(0), wspec(H_DN * DN_DK), wspec(2 * H_DN * DN_DK), row, row, row],
        out_specs=[pl.BlockSpec((TT, W), lambda b, hg, t: (b * NT + t, hg)),
                   pl.BlockSpec((None, HG, DN_DK, DN_DV), lambda b, hg, t: (b, hg, 0, 0))],
        out_shape=[jax.ShapeDtypeStruct((M, H_DN * DN_DV), BF16),
                   jax.ShapeDtypeStruct((B, H_DN, DN_DK, DN_DV), F32)],
        scratch_shapes=[pltpu.VMEM(shape, dtype) for shape, dtype in scratch],
        compiler_params=pltpu.CompilerParams(
            dimension_semantics=("parallel", "parallel", "arbitrary"),
            vmem_limit_bytes=_vmem_limit(blocks, scratch + temporaries)),
        name=name,
    )(zb, zb, zb, zb, ba, conv_w, conv_w, conv_w, alog_row, dtb_row, gain_dn.reshape(1, LANES))


def _lambda(lam_ref, lam_init):
    lv = lam_ref[...]
    s1 = jnp.sum(lv[0:1, :] * lv[1:2, :], axis=-1, keepdims=True)
    s2 = jnp.sum(lv[2:3, :] * lv[3:4, :], axis=-1, keepdims=True)
    return jnp.exp(s1) - jnp.exp(s2) + lam_init


def _attn_kernel(zq_ref, zk_ref, zv_ref, diag_ref, sub_ref, lam_ref, gain_ref, o_ref, kb_s, vb_s, *, tq, lam_init):
    T = zk_ref.shape[0]
    kb_s[...] = zk_ref[...].astype(BF16)
    vb_s[...] = zv_ref[...].astype(BF16)
    lam = _lambda(lam_ref, lam_init)
    lane = lax.broadcasted_iota(jnp.int32, (tq, LANES), 1)
    for qi in range(T // tq):
        rows = slice(qi * tq, (qi + 1) * tq)
        n_keys = (qi + 1) * tq
        q = zq_ref[rows, :].astype(F32) * (DIFF_DQ ** -0.5)
        ks = kb_s[0:n_keys, :]
        vs = vb_s[0:n_keys, :]
        outs = []
        for mi in range(2):
            in_map = (lane >= DIFF_DQ) if mi else (lane < DIFF_DQ)
            qm = jnp.where(in_map, q, 0.0).astype(BF16)
            s = lax.dot_general(qm, ks, (((1,), (1,)), ((), ())), preferred_element_type=F32)
            parts = []
            if qi >= 2:
                parts.append(s[:, :(qi - 1) * tq])
            if qi >= 1:
                parts.append(s[:, (qi - 1) * tq:qi * tq] + sub_ref[...])
            parts.append(s[:, qi * tq:] + diag_ref[...])
            s = parts[0] if len(parts) == 1 else jnp.concatenate(parts, axis=1)
            p = jnp.exp(s - jnp.max(s, axis=-1, keepdims=True))
            l = jnp.sum(p, axis=-1, keepdims=True)
            outs.append(jnp.dot(p.astype(BF16), vs, preferred_element_type=F32) / l)
        o = outs[0] - lam * outs[1]
        o_ref[rows, :] = (_rms(o, gain_ref[...]) * (1.0 - lam_init)).astype(o_ref.dtype)


def _attn_prompt(zb, kf, vf, diag, sub, lam_rows, gain_df, B, T, lam_init, *, tq, name):
    M = zb.shape[0]
    assert T % tq == 0
    row = pl.BlockSpec((1, LANES), lambda b, h: (0, 0))
    blocks = [((T, LANES), zb.dtype)] + [((T, LANES), F32)] * 2 + [((tq, tq), F32)] * 2 + [((T, LANES), BF16)]
    resident = [((T, LANES), BF16)] * 2 + [((tq, T), F32)] * 4
    return pl.pallas_call(
        functools.partial(_attn_kernel, tq=tq, lam_init=lam_init),
        grid=(B, H_DIFF),
        in_specs=[pl.BlockSpec((T, LANES), lambda b, h: (b, _C_QDF // LANES + h)),
                  pl.BlockSpec((T, LANES), lambda b, h: (b, h)),
                  pl.BlockSpec((T, LANES), lambda b, h: (b, h)),
                  pl.BlockSpec((None, tq, tq), lambda b, h: (h, 0, 0)),
                  pl.BlockSpec((None, tq, tq), lambda b, h: (h, 0, 0)),
                  pl.BlockSpec((SUBLANES, LANES), lambda b, h: (0, 0)), row],
        out_specs=pl.BlockSpec((T, LANES), lambda b, h: (b, h)),
        out_shape=jax.ShapeDtypeStruct((M, H_DIFF * DIFF_DV), BF16),
        scratch_shapes=[pltpu.VMEM((T, LANES), BF16)] * 2,
        compiler_params=pltpu.CompilerParams(
            dimension_semantics=("parallel", "parallel"),
            vmem_limit_bytes=_vmem_limit(blocks, resident)),
        name=name,
    )(zb, kf, vf, diag, sub, lam_rows, gain_df.reshape(1, LANES))


def _delta_step_kernel(zq_ref, zk_ref, zv_ref, zg_ref, ba_ref, cq_ref, ck_ref, cv_ref, wq_ref, wk_ref, wv_ref,
                       alog_ref, dtb_ref, gain_ref, s_ref, o_ref, sout_ref):
    def conv(z_ref, c_ref, w_ref):
        acc = z_ref[...] * w_ref[CONV_W - 1:CONV_W, :]
        for j in range(CONV_W - 1):
            acc = acc + c_ref[j:j + 1, :] * w_ref[j:j + 1, :]
        return _silu(acc)

    cq, ck, cv = conv(zq_ref, cq_ref, wq_ref), conv(zk_ref, ck_ref, wk_ref), conv(zv_ref, cv_ref, wv_ref)
    ba = ba_ref[...]
    bt = jax.nn.sigmoid(ba)
    gt = -jnp.exp(alog_ref[...]) * _softplus(ba + dtb_ref[...])
    lane = lax.broadcasted_iota(jnp.int32, (1, LANES), 1)
    pad = jnp.zeros((SUBLANES - 2, LANES), F32)
    for h in range(H_DN):
        lanes = slice(h * LANES, (h + 1) * LANES)
        q = cq[:, lanes]
        q = q * lax.rsqrt(jnp.sum(q * q, axis=-1, keepdims=True) + EPS) * (DN_DK ** -0.5)
        k = ck[:, lanes]
        k = k * lax.rsqrt(jnp.sum(k * k, axis=-1, keepdims=True) + EPS)
        v = cv[:, lanes]
        beta = jnp.sum(jnp.where(lane == h, bt, 0.0), axis=-1, keepdims=True)
        eg = jnp.exp(jnp.sum(jnp.where(lane == h + H_DN, gt, 0.0), axis=-1, keepdims=True))
        S = s_ref[h]
        r = _dot(jnp.concatenate([k, q, pad], axis=0), S)
        v_new = beta * (v - eg * r[0:1])
        o = eg * r[1:2] + jnp.sum(q * k, axis=-1, keepdims=True) * v_new
        zeros7 = jnp.zeros((SUBLANES - 1, LANES), F32)
        outer = _dot_tn(jnp.concatenate([k, zeros7], axis=0), jnp.concatenate([v_new, zeros7], axis=0))
        sout_ref[h] = S * eg + outer
        o_ref[:, lanes] = (_rms(o, gain_ref[...]) * _silu(zg_ref[:, lanes])).astype(o_ref.dtype)


def _delta_step(z3, ba3, conv_state, conv_w, alog_row, dtb_row, gain_dn, state, layer, *, name):
    DB = z3.shape[0]
    W = H_DN * LANES

    def zspec(col0):
        return pl.BlockSpec((None, 1, W), lambda b: (b, 0, col0 // W))

    def cspec(col0):
        return pl.BlockSpec((None, None, CONV_W - 1, W), lambda b: (layer, b, 0, col0 // W))

    def wspec(col0):
        return pl.BlockSpec((CONV_W, W), lambda b: (0, col0 // W))

    row = pl.BlockSpec((1, LANES), lambda b: (0, 0))
    return pl.pallas_call(
        _delta_step_kernel,
        grid=(DB,),
        in_specs=[zspec(_C_QDN), zspec(_C_KDN), zspec(_C_VDN), zspec(_C_GATE),
                  pl.BlockSpec((None, 1, LANES), lambda b: (b, 0, 0)),
                  cspec(0), cspec(W), cspec(2 * W), wspec(0), wspec(W), wspec(2 * W), row, row, row,
                  pl.BlockSpec((None, None, H_DN, DN_DK, DN_DV), lambda b: (layer, b, 0, 0, 0))],
        out_specs=[pl.BlockSpec((None, 1, W), lambda b: (b, 0, 0)),
                   pl.BlockSpec((None, H_DN, DN_DK, DN_DV), lambda b: (b, 0, 0, 0))],
        out_shape=[jax.ShapeDtypeStruct((DB, 1, W), F32),
                   jax.ShapeDtypeStruct((DB, H_DN, DN_DK, DN_DV), F32)],
        compiler_params=pltpu.CompilerParams(dimension_semantics=("parallel",)),
        name=name,
    )(z3, z3, z3, z3, ba3, conv_state, conv_state, conv_state, conv_w, conv_w, conv_w,
      alog_row, dtb_row, gain_dn.reshape(1, LANES), state)


def _decode_kernel(pt_ref, qt_ref, knew_ref, vnew_ref, bias_ref, lam_ref, gain_ref, *rest, G, lam_init):
    k_refs = rest[:G]
    v_refs = rest[G:2 * G]
    o_ref = rest[2 * G]
    m_s, l_s, acc_s = rest[2 * G + 1:]
    step = pl.program_id(1)
    nsteps = pl.num_programs(1)

    @pl.when(step == 0)
    def _():
        m_s[...] = jnp.full(m_s.shape, -jnp.inf, F32)
        l_s[...] = jnp.zeros_like(l_s)
        acc_s[...] = jnp.zeros_like(acc_s)

    qt = qt_ref[...]
    scores = []
    for i in range(G):
        s = lax.dot_general(qt, k_refs[i][...].astype(BF16), (((1,), (1,)), ((), ())),
                            preferred_element_type=F32)
        if i == G - 1:
            s = s + jnp.where(step == nsteps - 1, bias_ref[1], bias_ref[0])
        else:
            s = s + bias_ref[0]
        scores.append(s)
    m_old = m_s[...]
    m_new = m_old
    for s in scores:
        m_new = jnp.maximum(m_new, jnp.max(s, axis=-1, keepdims=True))
    alpha = jnp.exp(m_old - m_new)
    l_new = alpha * l_s[...]
    acc = alpha * acc_s[...]
    for i in range(G):
        p = jnp.exp(scores[i] - m_new)
        l_new = l_new + jnp.sum(p, axis=-1, keepdims=True)
        acc = acc + jnp.dot(p.astype(BF16), v_refs[i][...].astype(BF16), preferred_element_type=F32)
    m_s[...] = m_new
    l_s[...] = l_new
    acc_s[...] = acc

    @pl.when(step == nsteps - 1)
    def _():
        kn = knew_ref[...].astype(BF16).astype(F32)
        kn = jnp.concatenate([kn, kn], axis=0)
        vn = jnp.concatenate([vnew_ref[...], vnew_ref[...]], axis=0)
        s = jnp.sum(qt.astype(F32) * kn, axis=-1, keepdims=True) + bias_ref[2][:, 0:1]
        m_o = m_s[...]
        m_n = jnp.maximum(m_o, s)
        a = jnp.exp(m_o - m_n)
        p = jnp.exp(s - m_n)
        l_f = a * l_s[...] + p
        acc_f = (a * acc_s[...] + p * vn) / l_f
        o = acc_f[0:H_DIFF] - _lambda(lam_ref, lam_init) * acc_f[H_DIFF:2 * H_DIFF]
        o_ref[...] = _rms(o, gain_ref[...]) * (1.0 - lam_init)


def _decode_attn(qt, knew, vnew, dec_bias, lam_rows, gain_df, pages_k, pages_v, page_table, layer,
                 lam_init, *, G=8, name):
    DB, n_pages = page_table.shape
    PH = pages_k.shape[2]
    R = 2 * H_DIFF
    G = min(G, n_pages)
    assert n_pages % G == 0

    def pspec(i):
        return pl.BlockSpec((None, None, PH, LANES), lambda b, s, pt: (layer, pt[b, s * G + i], 0, 0))

    def per_b(rows):
        return pl.BlockSpec((None, rows, LANES), lambda b, s, pt: (b, 0, 0))

    row = pl.BlockSpec((1, LANES), lambda b, s, pt: (0, 0))
    blocks = [((PH, LANES), F32)] * (2 * G) + [((3, R, PH), F32)]
    grid_spec = pltpu.PrefetchScalarGridSpec(
        num_scalar_prefetch=1,
        grid=(DB, n_pages // G),
        in_specs=[per_b(R), per_b(H_DIFF), per_b(H_DIFF),
                  pl.BlockSpec((3, R, PH), lambda b, s, pt: (0, 0, 0)),
                  pl.BlockSpec((SUBLANES, LANES), lambda b, s, pt: (0, 0)), row]
        + [pspec(i) for i in range(G)] + [pspec(i) for i in range(G)],
        out_specs=per_b(H_DIFF),
        scratch_shapes=[pltpu.VMEM((R, 1), F32), pltpu.VMEM((R, 1), F32), pltpu.VMEM((R, LANES), F32)],
    )
    return pl.pallas_call(
        functools.partial(_decode_kernel, G=G, lam_init=lam_init),
        grid_spec=grid_spec,
        out_shape=jax.ShapeDtypeStruct((DB, H_DIFF, LANES), F32),
        compiler_params=pltpu.CompilerParams(
            dimension_semantics=("parallel", "arbitrary"), vmem_limit_bytes=_vmem_limit(blocks)),
        name=name,
    )(page_table, qt, knew, vnew, dec_bias, lam_rows, gain_df.reshape(1, LANES),
      *([pages_k] * G), *([pages_v] * G))


def _pad_rows(x, rows):
    return jnp.pad(x, ((0, rows - x.shape[0]), (0, 0)))


def kernel(x_prompt, x_sample, cache_k, cache_v, state_delta, state_conv, page_table, rel_bias, norm_mix, w_in, conv_w, a_log, dt_bias, norm_dn, w_o_dn, lam_q1, lam_k1, lam_q2, lam_k2, norm_diff, w_o_diff, w_out, norm_mlp, w_up, w_down, norm_final):
    B, T, D = x_prompt.shape
    DB = x_sample.shape[0]
    depth = w_in.shape[0]
    n_pool, page = cache_k.shape[1], cache_k.shape[2]
    conv_dim = conv_w.shape[-1]
    n_ba = 2 * H_DN
    assert conv_dim == 3 * H_DN * DN_DK and x_sample.shape[1] == 1
    c_gate = conv_dim + n_ba
    c_kdf = c_gate + H_DN * DN_DV + H_DIFF * 2 * DIFF_DQ
    c_gbr = c_kdf + H_DIFF * 2 * DIFF_DQ + H_DIFF * DIFF_DV
    f_widths = (H_DIFF * 2 * DIFF_DQ, H_DIFF * DIFF_DV, LANES)
    MS = 16
    tq = min(256, T)

    xp = x_prompt.reshape(B * T, D)
    xs = _pad_rows(x_sample.reshape(DB, D), MS)
    diag, sub = _bias_tiles(rel_bias, tq, tq)
    dec_bias = _dec_bias(rel_bias, page)
    pages_k = cache_k.reshape(depth, n_pool, page * H_DIFF, 2 * DIFF_DQ)
    pages_v = cache_v.reshape(depth, n_pool, page * H_DIFF, DIFF_DV)
    in_map0 = jnp.arange(2 * DIFF_DQ) < DIFF_DQ

    kp_l, vp_l, sp_l, cp_l, ks_l, vs_l, ss_l, cs_l = [], [], [], [], [], [], [], []
    for l in range(depth):
        lam_init = 0.8 - 0.6 * math.exp(-0.3 * l)
        wl = w_in[l]
        w_b = jnp.concatenate([wl[:, :conv_dim], wl[:, c_gate:c_kdf], wl[:, c_gbr:]], axis=1).astype(BF16)
        w_f = jnp.pad(jnp.concatenate([wl[:, c_kdf:c_gbr], wl[:, conv_dim:c_gate]], axis=1),
                      ((0, 0), (0, LANES - n_ba))).astype(BF16)
        w_odn, w_odf, w_o = w_o_dn[l].astype(BF16), w_o_diff[l].astype(BF16), w_out[l].astype(BF16)
        w_u, w_d = w_up[l].astype(BF16), w_down[l].astype(BF16)
        alog_row = jnp.pad(a_log[l], (H_DN, LANES - n_ba)).reshape(1, LANES)
        dtb_row = jnp.pad(dt_bias[l], (H_DN, LANES - n_ba)).reshape(1, LANES)
        lam_rows = jnp.pad(jnp.stack([lam_q1[l], lam_k1[l], lam_q2[l], lam_k2[l]]),
                           ((0, SUBLANES - 4), (0, LANES - DIFF_DQ)))
        last = l == depth - 1

        zb = _mm(xp, w_b, gain=norm_mix[l], out_dtype=BF16, name=f"in_proj_p{l}")
        kf, vf, ba = _mm_split(xp, w_f, norm_mix[l], f_widths, name=f"in_proj_kv_p{l}")
        x_tail = _pad_rows(xp.reshape(B, T, D)[:, T - (CONV_W - 1):].reshape(B * (CONV_W - 1), D), MS)
        z_tail = _mm(x_tail, w_b, gain=norm_mix[l], tn=1024, name=f"conv_tail_p{l}")
        o_dn, s_fin = _delta_prompt(zb, ba, conv_w[l], alog_row, dtb_row, norm_dn[l], B, T, name=f"delta_p{l}")
        o_df = _attn_prompt(zb, kf, vf, diag, sub, lam_rows, norm_diff[l], B, T, lam_init, tq=tq,
                            name=f"attn_p{l}")
        merged = _merge(o_dn, o_df, zb, w_odn, w_odf, name=f"merge_p{l}")
        xp = _mm(merged, w_o, residual=xp, name=f"out_proj_p{l}")
        xp = _mlp(xp, norm_mlp[l], w_u, w_d, final_gain=norm_final if last else None, name=f"mlp_p{l}")
        kp_l.append(kf.reshape(B, T, H_DIFF, 2 * DIFF_DQ))
        vp_l.append(vf.reshape(B, T, H_DIFF, DIFF_DV))
        sp_l.append(s_fin)
        cp_l.append(z_tail[:B * (CONV_W - 1), :conv_dim].reshape(B, CONV_W - 1, conv_dim))

        zs = _mm(xs, w_b, gain=norm_mix[l], tn=1024, name=f"in_proj_s{l}")
        ks16, vs16, bas16 = _mm_split(xs, w_f, norm_mix[l], f_widths, name=f"in_proj_kv_s{l}")
        zs3 = zs[:DB].reshape(DB, 1, -1)
        o_dn_s, s_new = _delta_step(zs3, bas16[:DB].reshape(DB, 1, LANES), state_conv,
                                    conv_w[l], alog_row, dtb_row, norm_dn[l], state_delta, l, name=f"delta_s{l}")
        q_s = zs[:DB, _C_QDF:_C_QDF + 1024].reshape(DB, H_DIFF, 2 * DIFF_DQ) * (DIFF_DQ ** -0.5)
        qt = jnp.concatenate([jnp.where(in_map0, q_s, 0.0), jnp.where(in_map0, 0.0, q_s)], axis=1).astype(BF16)
        k_s = ks16[:DB].reshape(DB, 1, -1)
        v_s = vs16[:DB].reshape(DB, 1, -1)
        o_df_s = _decode_attn(qt, k_s.reshape(DB, H_DIFF, 2 * DIFF_DQ), v_s.reshape(DB, H_DIFF, DIFF_DV),
                              dec_bias, lam_rows, norm_diff[l], pages_k, pages_v, page_table, l,
                              lam_init, name=f"attn_s{l}")
        merged_s = _merge(_pad_rows(o_dn_s.reshape(DB, -1), MS).astype(BF16),
                          _pad_rows(o_df_s.reshape(DB, -1), MS).astype(BF16), zs,
                          w_odn, w_odf, tn=1024, name=f"merge_s{l}")
        xs = _mm(merged_s, w_o, residual=xs, tn=1024, name=f"out_proj_s{l}")
        xs = _mlp(xs, norm_mlp[l], w_u, w_d, final_gain=norm_final if last else None, name=f"mlp_s{l}")
        ks_l.append(k_s.reshape(DB, 1, H_DIFF, 2 * DIFF_DQ))
        vs_l.append(v_s.reshape(DB, 1, H_DIFF, DIFF_DV))
        ss_l.append(s_new)
        cs_l.append(jnp.concatenate([state_conv[l][:, 1:], zs3[:, :, :conv_dim]], axis=1))

    y_prompt = xp.reshape(B, T, D)
    y_sample = xs[:DB].reshape(DB, 1, D)
    return (y_prompt, y_sample,
            jnp.stack(kp_l), jnp.stack(vp_l), jnp.stack(sp_l), jnp.stack(cp_l),
            jnp.stack(ks_l), jnp.stack(vs_l), jnp.stack(ss_l), jnp.stack(cs_l))
```

```python
import functools
import math

import jax
import jax.numpy as jnp
from jax import lax
from jax.experimental import pallas as pl
from jax.experimental.pallas import tpu as pltpu

F32 = jnp.float32
BF16 = jnp.bfloat16

H_DN = 8
DN_DK = 128
DN_DV = 128
CONV_W = 4
H_DIFF = 8
DIFF_DQ = 64
DIFF_DV = 128
NUM_BUCKETS = 32
MAX_DISTANCE = 128
EPS = 1e-6

LANES = 128
SUBLANES = 8
VMEM_BYTES_V7X = 64 * 1024 * 1024
VMEM_INTERNAL_ALLOWANCE = 12 * 1024 * 1024

_MAX_EXACT = NUM_BUCKETS // 2
_BUCKET_START = tuple(
    k if k <= _MAX_EXACT else math.ceil(
        _MAX_EXACT * (MAX_DISTANCE / _MAX_EXACT) ** ((k - _MAX_EXACT) / (NUM_BUCKETS - _MAX_EXACT)))
    for k in range(NUM_BUCKETS))
_FAR_REL = _BUCKET_START[-1]

_C_QDN, _C_KDN, _C_VDN, _C_GATE = 0, 1024, 2048, 3072
_C_QDF = 4096
_C_GA = 5120


def _nbytes(shape, dtype):
    return math.prod(shape) * jnp.dtype(dtype).itemsize


def _vmem_limit(pipelined, resident=()):
    need = 2 * sum(_nbytes(s, d) for s, d in pipelined) + sum(_nbytes(s, d) for s, d in resident)
    return int(min(need + VMEM_INTERNAL_ALLOWANCE, VMEM_BYTES_V7X - 4 * 1024 * 1024))


def _dot(a, b):
    return jnp.dot(a.astype(BF16), b.astype(BF16), preferred_element_type=F32)


def _dot_nt(a, b):
    return lax.dot_general(a.astype(BF16), b.astype(BF16), (((1,), (1,)), ((), ())),
                           preferred_element_type=F32)


def _dot_tn(a, b):
    return lax.dot_general(a.astype(BF16), b.astype(BF16), (((0,), (0,)), ((), ())),
                           preferred_element_type=F32)


def _bdot(a, b):
    return lax.dot_general(a.astype(BF16), b.astype(BF16), (((2,), (1,)), ((0,), (0,))),
                           preferred_element_type=F32)


def _bdot_nt(a, b):
    return lax.dot_general(a.astype(BF16), b.astype(BF16), (((2,), (2,)), ((0,), (0,))),
                           preferred_element_type=F32)


def _bdot_tn(a, b):
    return lax.dot_general(a.astype(BF16), b.astype(BF16), (((1,), (1,)), ((0,), (0,))),
                           preferred_element_type=F32)


def _rms(x, gain):
    return x * lax.rsqrt(jnp.mean(x * x, axis=-1, keepdims=True) + EPS) * gain


def _silu(x):
    h = 0.5 * x
    return h + h * jnp.tanh(h)


def _softplus(x):
    return jnp.maximum(x, 0.0) + jnp.log1p(jnp.exp(-jnp.abs(x)))


def _norm_rows_to(x_ref, g_ref, h_ref):
    tm = x_ref.shape[0]
    rows = min(tm, 128)

    def body(i, c):
        sl = pl.ds(pl.multiple_of(i * rows, rows), rows)
        h_ref[sl, :] = _rms(x_ref[sl, :], g_ref[...]).astype(h_ref.dtype)
        return c

    lax.fori_loop(0, tm // rows, body, 0)


def _mm_kernel(*refs, has_gain, has_res):
    it = iter(refs)
    x_ref = next(it)
    g_ref = next(it) if has_gain else None
    w_ref = next(it)
    r_ref = next(it) if has_res else None
    o_ref = next(it)
    if has_gain:
        h_ref = next(it)

        @pl.when(pl.program_id(1) == 0)
        def _():
            _norm_rows_to(x_ref, g_ref, h_ref)

        lhs = h_ref[...]
    else:
        lhs = x_ref[...]
    acc = jnp.dot(lhs, w_ref[...], preferred_element_type=F32)
    if has_res:
        acc = acc + r_ref[...]
    o_ref[...] = acc.astype(o_ref.dtype)


def _mm(x, w, *, gain=None, residual=None, out_dtype=F32, tm=1024, tn=512, name):
    M, K = x.shape
    N = w.shape[1]
    tm, tn = min(tm, M), min(tn, N)
    assert M % tm == 0 and N % tn == 0, (M, N, tm, tn)
    in_specs = [pl.BlockSpec((tm, K), lambda i, j: (i, 0))]
    args = [x]
    blocks = [((tm, K), x.dtype), ((K, tn), w.dtype), ((tm, tn), out_dtype)]
    scratch = []
    if gain is not None:
        in_specs.append(pl.BlockSpec((1, K), lambda i, j: (0, 0)))
        args.append(gain.reshape(1, K))
        scratch.append(pltpu.VMEM((tm, K), BF16))
    in_specs.append(pl.BlockSpec((K, tn), lambda i, j: (0, j)))
    args.append(w)
    if residual is not None:
        in_specs.append(pl.BlockSpec((tm, tn), lambda i, j: (i, j)))
        args.append(residual)
        blocks.append(((tm, tn), residual.dtype))
    return pl.pallas_call(
        functools.partial(_mm_kernel, has_gain=gain is not None, has_res=residual is not None),
        grid=(M // tm, N // tn),
        in_specs=in_specs,
        out_specs=pl.BlockSpec((tm, tn), lambda i, j: (i, j)),
        out_shape=jax.ShapeDtypeStruct((M, N), out_dtype),
        scratch_shapes=scratch,
        compiler_params=pltpu.CompilerParams(
            dimension_semantics=("parallel", "arbitrary"),
            vmem_limit_bytes=_vmem_limit(blocks, [((tm, K), BF16)] if gain is not None else [])),
        name=name,
    )(*args)


def _mm_split_kernel(x_ref, g_ref, w_ref, *rest, widths):
    out_refs, h_ref = rest[:-1], rest[-1]
    _norm_rows_to(x_ref, g_ref, h_ref)
    acc = jnp.dot(h_ref[...], w_ref[...], preferred_element_type=F32)
    col = 0
    for o_ref, width in zip(out_refs, widths):
        o_ref[...] = acc[:, col:col + width]
        col += width


def _mm_split(x, w, gain, widths, *, tm=512, name):
    M, K = x.shape
    N = w.shape[1]
    tm = min(tm, M)
    assert M % tm == 0 and sum(widths) == N and all(wd % LANES == 0 for wd in widths)
    blocks = [((tm, K), x.dtype), ((K, N), w.dtype), ((tm, N), F32)]
    return pl.pallas_call(
        functools.partial(_mm_split_kernel, widths=tuple(widths)),
        grid=(M // tm,),
        in_specs=[pl.BlockSpec((tm, K), lambda i: (i, 0)), pl.BlockSpec((1, K), lambda i: (0, 0)),
                  pl.BlockSpec((K, N), lambda i: (0, 0))],
        out_specs=[pl.BlockSpec((tm, wd), lambda i: (i, 0)) for wd in widths],
        out_shape=[jax.ShapeDtypeStruct((M, wd), F32) for wd in widths],
        scratch_shapes=[pltpu.VMEM((tm, K), BF16)],
        compiler_params=pltpu.CompilerParams(
            dimension_semantics=("parallel",),
            vmem_limit_bytes=_vmem_limit(blocks, [((tm, K), BF16), ((tm, N), F32)])),
        name=name,
    )(x, gain.reshape(1, K), w)


def _merge_kernel(odn_ref, odf_ref, ga_ref, gb_ref, wdn_ref, wdf_ref, o_ref):
    a = jnp.dot(odn_ref[...], wdn_ref[...], preferred_element_type=F32)
    b = jnp.dot(odf_ref[...], wdf_ref[...], preferred_element_type=F32)
    ga = jax.nn.sigmoid(ga_ref[...].astype(F32))
    gb = jax.nn.sigmoid(gb_ref[...].astype(F32))
    o_ref[...] = (ga * a + gb * b).astype(o_ref.dtype)


def _merge(o_dn, o_df, z, w_dn, w_df, *, tm=1024, tn=1024, name):
    M, K = o_dn.shape
    N = w_dn.shape[1]
    tm, tn = min(tm, M), min(tn, N)
    assert M % tm == 0 and N % tn == 0 and _C_GA % tn == 0
    ga0 = _C_GA // tn
    gb0 = (_C_GA + N) // tn
    blocks = [((tm, K), BF16)] * 2 + [((tm, tn), z.dtype)] * 2 + [((K, tn), BF16)] * 2 + [((tm, tn), BF16)]
    return pl.pallas_call(
        _merge_kernel,
        grid=(M // tm, N // tn),
        in_specs=[
            pl.BlockSpec((tm, K), lambda i, j: (i, 0)),
            pl.BlockSpec((tm, K), lambda i, j: (i, 0)),
            pl.BlockSpec((tm, tn), lambda i, j: (i, ga0 + j)),
            pl.BlockSpec((tm, tn), lambda i, j: (i, gb0 + j)),
            pl.BlockSpec((K, tn), lambda i, j: (0, j)),
            pl.BlockSpec((K, tn), lambda i, j: (0, j)),
        ],
        out_specs=pl.BlockSpec((tm, tn), lambda i, j: (i, j)),
        out_shape=jax.ShapeDtypeStruct((M, N), BF16),
        compiler_params=pltpu.CompilerParams(
            dimension_semantics=("parallel", "arbitrary"), vmem_limit_bytes=_vmem_limit(blocks)),
        name=name,
    )(o_dn, o_df, z, z, w_dn, w_df)


def _mlp_kernel(*refs, has_final):
    if has_final:
        x_ref, g_ref, wu_ref, wd_ref, gf_ref, o_ref, h_ref, acc_ref = refs
    else:
        x_ref, g_ref, wu_ref, wd_ref, o_ref, h_ref, acc_ref = refs
        gf_ref = None
    f = pl.program_id(1)

    @pl.when(f == 0)
    def _():
        _norm_rows_to(x_ref, g_ref, h_ref)
        acc_ref[...] = jnp.zeros_like(acc_ref)

    u = jnp.dot(h_ref[...], wu_ref[...], preferred_element_type=F32)
    u = jnp.square(jnp.maximum(u, 0.0)).astype(BF16)
    acc_ref[...] += jnp.dot(u, wd_ref[...], preferred_element_type=F32)

    @pl.when(f == pl.num_programs(1) - 1)
    def _():
        y = x_ref[...] + acc_ref[...]
        if has_final:
            y = _rms(y, gf_ref[...])
        o_ref[...] = y


def _mlp(x, gain, w_up, w_down, *, final_gain=None, tm=512, tf=1024, name):
    M, D = x.shape
    FF = w_up.shape[1]
    tm, tf = min(tm, M), min(tf, FF)
    assert M % tm == 0 and FF % tf == 0
    in_specs = [
        pl.BlockSpec((tm, D), lambda i, f: (i, 0)),
        pl.BlockSpec((1, D), lambda i, f: (0, 0)),
        pl.BlockSpec((D, tf), lambda i, f: (0, f)),
        pl.BlockSpec((tf, D), lambda i, f: (f, 0)),
    ]
    args = [x, gain.reshape(1, D), w_up, w_down]
    if final_gain is not None:
        in_specs.append(pl.BlockSpec((1, D), lambda i, f: (0, 0)))
        args.append(final_gain.reshape(1, D))
    blocks = [((tm, D), F32), ((D, tf), BF16), ((tf, D), BF16), ((tm, D), F32)]
    resident = [((tm, D), BF16), ((tm, D), F32)]
    return pl.pallas_call(
        functools.partial(_mlp_kernel, has_final=final_gain is not None),
        grid=(M // tm, FF // tf),
        in_specs=in_specs,
        out_specs=pl.BlockSpec((tm, D), lambda i, f: (i, 0)),
        out_shape=jax.ShapeDtypeStruct((M, D), F32),
        scratch_shapes=[pltpu.VMEM((tm, D), BF16), pltpu.VMEM((tm, D), F32)],
        compiler_params=pltpu.CompilerParams(
            dimension_semantics=("parallel", "arbitrary"), vmem_limit_bytes=_vmem_limit(blocks, resident)),
        name=name,
    )(*args)


def _bias_of_rel(rb_ref, h, rel):
    far = rb_ref[NUM_BUCKETS - 1, h]
    v = jnp.full(rel.shape, rb_ref[0, h] - far, F32)
    for k in range(1, NUM_BUCKETS):
        v = jnp.where(rel >= _BUCKET_START[k], rb_ref[k, h] - far, v)
    return v


def _bias_tiles_kernel(rb_ref, diag_ref, sub_ref, *, tk):
    h = pl.program_id(0)
    shape = diag_ref.shape
    rel = lax.broadcasted_iota(jnp.int32, shape, 0) - lax.broadcasted_iota(jnp.int32, shape, 1)
    diag_ref[...] = jnp.where(rel >= 0, _bias_of_rel(rb_ref, h, rel), -jnp.inf)
    sub_ref[...] = _bias_of_rel(rb_ref, h, rel + tk)


def _bias_tiles(rel_bias, tq, tk):
    assert tq == tk and tk + 1 >= _FAR_REL
    H = rel_bias.shape[1]
    return pl.pallas_call(
        functools.partial(_bias_tiles_kernel, tk=tk),
        grid=(H,),
        in_specs=[pl.BlockSpec(memory_space=pltpu.SMEM)],
        out_specs=[pl.BlockSpec((None, tq, tk), lambda h: (h, 0, 0))] * 2,
        out_shape=[jax.ShapeDtypeStruct((H, tq, tk), F32)] * 2,
        name="t5_bias_tiles",
    )(rel_bias)


def _dec_bias_kernel(rb_ref, o_ref, *, page):
    shape = o_ref.shape[1:]
    row_head = lax.broadcasted_iota(jnp.int32, shape, 0) & (H_DIFF - 1)
    lane = lax.broadcasted_iota(jnp.int32, shape, 1)
    own_head = (lane & (H_DIFF - 1)) == row_head
    rel = page - (lane >> (H_DIFF.bit_length() - 1))
    last = jnp.zeros(shape, F32)
    new = jnp.zeros(shape, F32)
    for h in range(H_DIFF):
        sel = row_head == h
        last = jnp.where(sel, _bias_of_rel(rb_ref, h, rel), last)
        new = jnp.where(sel, rb_ref[0, h] - rb_ref[NUM_BUCKETS - 1, h], new)
    o_ref[0] = jnp.where(own_head, 0.0, -jnp.inf)
    o_ref[1] = jnp.where(own_head, last, -jnp.inf)
    o_ref[2] = new


def _dec_bias(rel_bias, page):
    assert page + 1 >= _FAR_REL and H_DIFF & (H_DIFF - 1) == 0
    return pl.pallas_call(
        functools.partial(_dec_bias_kernel, page=page),
        in_specs=[pl.BlockSpec(memory_space=pltpu.SMEM)],
        out_specs=pl.BlockSpec(memory_space=pltpu.VMEM),
        out_shape=jax.ShapeDtypeStruct((3, 2 * H_DIFF, page * H_DIFF), F32),
        name="t5_bias_decode",
    )(rel_bias)


def _unit_lower_inverse(L, ri, ci):
    C = L.shape[-1]
    eye = (ri == ci).astype(F32)
    T = None
    lb = 0
    while (1 << lb) < C:
        below = ((ri >> (lb + 1)) == (ci >> (lb + 1))) & (((ri >> lb) & 1) == 1) & (((ci >> lb) & 1) == 0)
        Lb = jnp.where(below, L, 0.0)
        T = eye - Lb if T is None else T - _bdot(_bdot(T, Lb), T)
        lb += 1
    return T


def _delta_chunk_local(q, k, v, beta, g):
    N, C, _ = q.shape
    ri = lax.broadcasted_iota(jnp.int32, (N, C, C), 1)
    ci = lax.broadcasted_iota(jnp.int32, (N, C, C), 2)
    tril = ri >= ci
    g_cols = jnp.broadcast_to(g, (N, C, C))
    g_row = jnp.sum(jnp.where(ri == ci, g_cols, 0.0), axis=1, keepdims=True)
    cum_col = jnp.sum(jnp.where(tril, jnp.broadcast_to(g_row, (N, C, C)), 0.0), axis=2, keepdims=True)
    cum_row = jnp.sum(jnp.where(ri <= ci, g_cols, 0.0), axis=1, keepdims=True)
    decay = jnp.where(tril, jnp.exp(jnp.where(tril, cum_col - cum_row, 0.0)), 0.0)
    g_last = jnp.sum(g, axis=1, keepdims=True)
    e_col = jnp.exp(cum_col)
    kb = k * beta
    vb = v * beta
    L = jnp.where(ri > ci, _bdot_nt(kb, k) * decay, 0.0)
    T = _unit_lower_inverse(L, ri, ci)
    vk = _bdot(T, jnp.concatenate([vb, kb * e_col], axis=2))
    qk = _bdot_nt(q, k) * decay
    return (vk[:, :, :DN_DV], vk[:, :, DN_DV:], q * e_col, qk, k * jnp.exp(g_last - cum_col),
            jnp.broadcast_to(jnp.exp(g_last), (N, SUBLANES, LANES)))


def _delta_kernel(zq_ref, zk_ref, zv_ref, zg_ref, ba_ref, wq_ref, wk_ref, wv_ref, alog_ref, dtb_ref,
                  gain_ref, o_ref, sout_ref, xq_s, xk_s, xv_s, q_s, k_s, v_s, S_s,
                  val_s, kcum_s, qe_s, qk_s, kdec_s, egl_s, *, HG, TT, C):
    hgi = pl.program_id(1)
    t = pl.program_id(2)
    W = HG * LANES
    NC = TT // C

    @pl.when(t == 0)
    def _():
        for x_s in (xq_s, xk_s, xv_s):
            x_s[0:SUBLANES, :] = jnp.zeros((SUBLANES, W), F32)
        S_s[...] = jnp.zeros_like(S_s)

    for z_ref, x_s, w_ref, dst in ((zq_ref, xq_s, wq_ref, q_s), (zk_ref, xk_s, wk_ref, k_s),
                                   (zv_ref, xv_s, wv_ref, v_s)):
        x_s[SUBLANES:SUBLANES + TT, :] = z_ref[...].astype(F32)
        xfull = x_s[...]
        acc = xfull[SUBLANES:, :] * w_ref[CONV_W - 1:CONV_W, :]
        for j in range(CONV_W - 1):
            shifted = pltpu.roll(xfull, CONV_W - 1 - j, axis=0)[SUBLANES:, :]
            acc = acc + shifted * w_ref[j:j + 1, :]
        dst[...] = _silu(acc)
        x_s[0:SUBLANES, :] = x_s[TT:TT + SUBLANES, :]

    ba = ba_ref[...]
    bt = jax.nn.sigmoid(ba)
    gt = -jnp.exp(alog_ref[...]) * _softplus(ba + dtb_ref[...])
    lane = lax.broadcasted_iota(jnp.int32, (TT, LANES), 1)
    qs, ks, vs, betas, gs = [], [], [], [], []
    for hh in range(HG):
        lanes = slice(hh * LANES, (hh + 1) * LANES)
        head = hgi * HG + hh
        qh = q_s[:, lanes]
        qh = qh * lax.rsqrt(jnp.sum(qh * qh, axis=-1, keepdims=True) + EPS) * (DN_DK ** -0.5)
        kh = k_s[:, lanes]
        kh = kh * lax.rsqrt(jnp.sum(kh * kh, axis=-1, keepdims=True) + EPS)
        qs.append(qh.reshape(NC, C, LANES))
        ks.append(kh.reshape(NC, C, LANES))
        vs.append(v_s[:, lanes].reshape(NC, C, LANES))
        betas.append(jnp.sum(jnp.where(lane == head, bt, 0.0), axis=-1, keepdims=True).reshape(NC, C, 1))
        gs.append(jnp.sum(jnp.where(lane == head + H_DN, gt, 0.0), axis=-1, keepdims=True).reshape(NC, C, 1))

    cat = functools.partial(jnp.concatenate, axis=0)
    local = _delta_chunk_local(cat(qs), cat(ks), cat(vs), cat(betas), cat(gs))
    for ref, val in zip((val_s, kcum_s, qe_s, qk_s, kdec_s, egl_s), local):
        ref[...] = val.reshape(ref.shape)

    for c in range(NC):
        rows = slice(c * C, (c + 1) * C)
        S = S_s[...]
        r = _bdot(jnp.concatenate([kcum_s[:, c], qe_s[:, c]], axis=1), S)
        v_new = val_s[:, c] - r[:, :C]
        o = r[:, C:] + _bdot(qk_s[:, c], v_new)
        S_s[...] = S * egl_s[:, c, 0:1, 0:1] + _bdot_tn(kdec_s[:, c], v_new)
        for hh in range(HG):
            lanes = slice(hh * LANES, (hh + 1) * LANES)
            gate = _silu(zg_ref[rows, lanes].astype(F32))
            o_ref[rows, lanes] = (_rms(o[hh], gain_ref[...]) * gate).astype(o_ref.dtype)

    @pl.when(t == pl.num_programs(2) - 1)
    def _():
        sout_ref[...] = S_s[...]


def _delta_prompt(zb, ba, conv_w, alog_row, dtb_row, gain_dn, B, T, *, HG=8, TT=256, C=64, name):
    M = zb.shape[0]
    TT = min(TT, T)
    C = min(C, TT)
    assert T % TT == 0 and TT % C == 0 and H_DN % HG == 0 and C & (C - 1) == 0 and C % 16 == 0
    NT = T // TT
    NC = TT // C
    W = HG * LANES
    nhg = H_DN // HG

    def zspec(col0):
        return pl.BlockSpec((TT, W), lambda b, hg, t: (b * NT + t, col0 // W + hg))

    def wspec(col0):
        return pl.BlockSpec((CONV_W, W), lambda b, hg, t: (0, col0 // W + hg))

    row = pl.BlockSpec((1, LANES), lambda b, hg, t: (0, 0))
    blocks = [((TT, W), zb.dtype)] * 4 + [((TT, LANES), F32)] + [((TT, W), BF16), ((HG, DN_DK, DN_DV), F32)]
    scratch = ([((TT + SUBLANES, W), F32)] * 3 + [((TT, W), F32)] * 3 + [((HG, DN_DK, DN_DV), F32)]
               + [((HG, NC, C, LANES), F32)] * 3 + [((HG, NC, C, C), F32)] + [((HG, NC, C, LANES), F32)]
               + [((HG, NC, SUBLANES, LANES), F32)])
    temporaries = [((HG * NC, C, LANES), F32)] * 12
    return pl.pallas_call(
        functools.partial(_delta_kernel, HG=HG, TT=TT, C=C),
        grid=(B, nhg, NT),
        in_specs=[zspec(_C_QDN), zspec(_C_KDN), zspec(_C_VDN), zspec(_C_GATE),
                  pl.BlockSpec((TT, LANES), lambda b, hg, t: (b * NT + t, 0)),
                  wspec(0), wspec(H_DN * DN_DK), wspec(2 * H_DN * DN_DK), row, row, row],
        out_specs=[pl.BlockSpec((TT, W), lambda b, hg, t: (b * NT + t, hg)),
                   pl.BlockSpec((None, HG, DN_DK, DN_DV), lambda b, hg, t: (b, hg, 0, 0))],
        out_shape=[jax.ShapeDtypeStruct((M, H_DN * DN_DV), BF16),
                   jax.ShapeDtypeStruct((B, H_DN, DN_DK, DN_DV), F32)],
        scratch_shapes=[pltpu.VMEM(shape, dtype) for shape, dtype in scratch],
        compiler_params=pltpu.CompilerParams(
            dimension_semantics=("parallel", "parallel", "arbitrary"),
            vmem_limit_bytes=_vmem_limit(blocks, scratch + temporaries)),
        name=name,
    )(zb, zb, zb, zb, ba, conv_w, conv_w, conv_w, alog_row, dtb_row, gain_dn.reshape(1, LANES))


def _lambda(lam_ref, lam_init):
    lv = lam_ref[...]
    s1 = jnp.sum(lv[0:1, :] * lv[1:2, :], axis=-1, keepdims=True)
    s2 = jnp.sum(lv[2:3, :] * lv[3:4, :], axis=-1, keepdims=True)
    return jnp.exp(s1) - jnp.exp(s2) + lam_init


def _attn_kernel(zq_ref, zk_ref, zv_ref, diag_ref, sub_ref, lam_ref, gain_ref, o_ref, kb_s, vb_s, *, tq, lam_init):
    T = zk_ref.shape[0]
    kb_s[...] = zk_ref[...].astype(BF16)
    vb_s[...] = zv_ref[...].astype(BF16)
    lam = _lambda(lam_ref, lam_init)
    lane = lax.broadcasted_iota(jnp.int32, (tq, LANES), 1)

    def scores(qi, mi):
        q = zq_ref[qi * tq:(qi + 1) * tq, :].astype(F32) * (DIFF_DQ ** -0.5)
        in_map = (lane >= DIFF_DQ) if mi else (lane < DIFF_DQ)
        qm = jnp.where(in_map, q, 0.0).astype(BF16)
        s = lax.dot_general(qm, kb_s[0:(qi + 1) * tq, :], (((1,), (1,)), ((), ())), preferred_element_type=F32)
        parts = []
        if qi >= 2:
            parts.append(s[:, :(qi - 1) * tq])
        if qi >= 1:
            parts.append(s[:, (qi - 1) * tq:qi * tq] + sub_ref[...])
        parts.append(s[:, qi * tq:] + diag_ref[...])
        return parts[0] if len(parts) == 1 else jnp.concatenate(parts, axis=1)

    items = [(qi, mi) for qi in range(T // tq) for mi in range(2)]
    s_next = scores(*items[0])
    outs = []
    for idx, (qi, mi) in enumerate(items):
        s = s_next
        if idx + 1 < len(items):
            s_next = scores(*items[idx + 1])
        p = jnp.exp(s - jnp.max(s, axis=-1, keepdims=True))
        l = jnp.sum(p, axis=-1, keepdims=True)
        outs.append(jnp.dot(p.astype(BF16), vb_s[0:(qi + 1) * tq, :], preferred_element_type=F32) / l)
        if mi == 1:
            o = outs[0] - lam * outs[1]
            outs = []
            o_ref[qi * tq:(qi + 1) * tq, :] = (_rms(o, gain_ref[...]) * (1.0 - lam_init)).astype(o_ref.dtype)


def _attn_prompt(zb, kf, vf, diag, sub, lam_rows, gain_df, B, T, lam_init, *, tq, name):
    M = zb.shape[0]
    assert T % tq == 0
    row = pl.BlockSpec((1, LANES), lambda b, h: (0, 0))
    blocks = [((T, LANES), zb.dtype)] + [((T, LANES), F32)] * 2 + [((tq, tq), F32)] * 2 + [((T, LANES), BF16)]
    resident = [((T, LANES), BF16)] * 2 + [((tq, T), F32)] * 4
    return pl.pallas_call(
        functools.partial(_attn_kernel, tq=tq, lam_init=lam_init),
        grid=(B, H_DIFF),
        in_specs=[pl.BlockSpec((T, LANES), lambda b, h: (b, _C_QDF // LANES + h)),
                  pl.BlockSpec((T, LANES), lambda b, h: (b, h)),
                  pl.BlockSpec((T, LANES), lambda b, h: (b, h)),
                  pl.BlockSpec((None, tq, tq), lambda b, h: (h, 0, 0)),
                  pl.BlockSpec((None, tq, tq), lambda b, h: (h, 0, 0)),
                  pl.BlockSpec((SUBLANES, LANES), lambda b, h: (0, 0)), row],
        out_specs=pl.BlockSpec((T, LANES), lambda b, h: (b, h)),
        out_shape=jax.ShapeDtypeStruct((M, H_DIFF * DIFF_DV), BF16),
        scratch_shapes=[pltpu.VMEM((T, LANES), BF16)] * 2,
        compiler_params=pltpu.CompilerParams(
            dimension_semantics=("parallel", "parallel"),
            vmem_limit_bytes=_vmem_limit(blocks, resident)),
        name=name,
    )(zb, kf, vf, diag, sub, lam_rows, gain_df.reshape(1, LANES))


def _delta_step_kernel(zq_ref, zk_ref, zv_ref, zg_ref, ba_ref, cq_ref, ck_ref, cv_ref, wq_ref, wk_ref, wv_ref,
                       alog_ref, dtb_ref, gain_ref, s_ref, o_ref, sout_ref):
    def conv(z_ref, c_ref, w_ref):
        acc = z_ref[...] * w_ref[CONV_W - 1:CONV_W, :]
        for j in range(CONV_W - 1):
            acc = acc + c_ref[j:j + 1, :] * w_ref[j:j + 1, :]
        return _silu(acc)

    cq, ck, cv = conv(zq_ref, cq_ref, wq_ref), conv(zk_ref, ck_ref, wk_ref), conv(zv_ref, cv_ref, wv_ref)
    ba = ba_ref[...]
    bt = jax.nn.sigmoid(ba)
    gt = -jnp.exp(alog_ref[...]) * _softplus(ba + dtb_ref[...])
    lane = lax.broadcasted_iota(jnp.int32, (1, LANES), 1)
    pad = jnp.zeros((SUBLANES - 2, LANES), F32)
    for h in range(H_DN):
        lanes = slice(h * LANES, (h + 1) * LANES)
        q = cq[:, lanes]
        q = q * lax.rsqrt(jnp.sum(q * q, axis=-1, keepdims=True) + EPS) * (DN_DK ** -0.5)
        k = ck[:, lanes]
        k = k * lax.rsqrt(jnp.sum(k * k, axis=-1, keepdims=True) + EPS)
        v = cv[:, lanes]
        beta = jnp.sum(jnp.where(lane == h, bt, 0.0), axis=-1, keepdims=True)
        eg = jnp.exp(jnp.sum(jnp.where(lane == h + H_DN, gt, 0.0), axis=-1, keepdims=True))
        S = s_ref[h]
        r = _dot(jnp.concatenate([k, q, pad], axis=0), S)
        v_new = beta * (v - eg * r[0:1])
        o = eg * r[1:2] + jnp.sum(q * k, axis=-1, keepdims=True) * v_new
        zeros7 = jnp.zeros((SUBLANES - 1, LANES), F32)
        outer = _dot_tn(jnp.concatenate([k, zeros7], axis=0), jnp.concatenate([v_new, zeros7], axis=0))
        sout_ref[h] = S * eg + outer
        o_ref[:, lanes] = (_rms(o, gain_ref[...]) * _silu(zg_ref[:, lanes])).astype(o_ref.dtype)


def _delta_step(z3, ba3, conv_state, conv_w, alog_row, dtb_row, gain_dn, state, layer, *, name):
    DB = z3.shape[0]
    W = H_DN * LANES

    def zspec(col0):
        return pl.BlockSpec((None, 1, W), lambda b: (b, 0, col0 // W))

    def cspec(col0):
        return pl.BlockSpec((None, None, CONV_W - 1, W), lambda b: (layer, b, 0, col0 // W))

    def wspec(col0):
        return pl.BlockSpec((CONV_W, W), lambda b: (0, col0 // W))

    row = pl.BlockSpec((1, LANES), lambda b: (0, 0))
    return pl.pallas_call(
        _delta_step_kernel,
        grid=(DB,),
        in_specs=[zspec(_C_QDN), zspec(_C_KDN), zspec(_C_VDN), zspec(_C_GATE),
                  pl.BlockSpec((None, 1, LANES), lambda b: (b, 0, 0)),
                  cspec(0), cspec(W), cspec(2 * W), wspec(0), wspec(W), wspec(2 * W), row, row, row,
                  pl.BlockSpec((None, None, H_DN, DN_DK, DN_DV), lambda b: (layer, b, 0, 0, 0))],
        out_specs=[pl.BlockSpec((None, 1, W), lambda b: (b, 0, 0)),
                   pl.BlockSpec((None, H_DN, DN_DK, DN_DV), lambda b: (b, 0, 0, 0))],
        out_shape=[jax.ShapeDtypeStruct((DB, 1, W), F32),
                   jax.ShapeDtypeStruct((DB, H_DN, DN_DK, DN_DV), F32)],
        compiler_params=pltpu.CompilerParams(dimension_semantics=("parallel",)),
        name=name,
    )(z3, z3, z3, z3, ba3, conv_state, conv_state, conv_state, conv_w, conv_w, conv_w,
      alog_row, dtb_row, gain_dn.reshape(1, LANES), state)


def _decode_kernel(pt_ref, qt_ref, knew_ref, vnew_ref, bias_ref, lam_ref, gain_ref, *rest, G, lam_init):
    k_refs = rest[:G]
    v_refs = rest[G:2 * G]
    o_ref = rest[2 * G]
    m_s, l_s, acc_s = rest[2 * G + 1:]
    step = pl.program_id(1)
    nsteps = pl.num_programs(1)

    @pl.when(step == 0)
    def _():
        m_s[...] = jnp.full(m_s.shape, -jnp.inf, F32)
        l_s[...] = jnp.zeros_like(l_s)
        acc_s[...] = jnp.zeros_like(acc_s)

    qt = qt_ref[...]
    scores = []
    for i in range(G):
        s = lax.dot_general(qt, k_refs[i][...].astype(BF16), (((1,), (1,)), ((), ())),
                            preferred_element_type=F32)
        if i == G - 1:
            s = s + jnp.where(step == nsteps - 1, bias_ref[1], bias_ref[0])
        else:
            s = s + bias_ref[0]
        scores.append(s)
    m_old = m_s[...]
    m_new = m_old
    for s in scores:
        m_new = jnp.maximum(m_new, jnp.max(s, axis=-1, keepdims=True))
    alpha = jnp.exp(m_old - m_new)
    l_new = alpha * l_s[...]
    acc = alpha * acc_s[...]
    for i in range(G):
        p = jnp.exp(scores[i] - m_new)
        l_new = l_new + jnp.sum(p, axis=-1, keepdims=True)
        acc = acc + jnp.dot(p.astype(BF16), v_refs[i][...].astype(BF16), preferred_element_type=F32)
    m_s[...] = m_new
    l_s[...] = l_new
    acc_s[...] = acc

    @pl.when(step == nsteps - 1)
    def _():
        kn = knew_ref[...].astype(BF16).astype(F32)
        kn = jnp.concatenate([kn, kn], axis=0)
        vn = jnp.concatenate([vnew_ref[...], vnew_ref[...]], axis=0)
        s = jnp.sum(qt.astype(F32) * kn, axis=-1, keepdims=True) + bias_ref[2][:, 0:1]
        m_o = m_s[...]
        m_n = jnp.maximum(m_o, s)
        a = jnp.exp(m_o - m_n)
        p = jnp.exp(s - m_n)
        l_f = a * l_s[...] + p
        acc_f = (a * acc_s[...] + p * vn) / l_f
        o = acc_f[0:H_DIFF] - _lambda(lam_ref, lam_init) * acc_f[H_DIFF:2 * H_DIFF]
        o_ref[...] = _rms(o, gain_ref[...]) * (1.0 - lam_init)


def _decode_attn(qt, knew, vnew, dec_bias, lam_rows, gain_df, pages_k, pages_v, page_table, layer,
                 lam_init, *, G=16, name):
    DB, n_pages = page_table.shape
    PH = pages_k.shape[2]
    R = 2 * H_DIFF
    G = min(G, n_pages)
    assert n_pages % G == 0

    def pspec(i):
        return pl.BlockSpec((None, None, PH, LANES), lambda b, s, pt: (layer, pt[b, s * G + i], 0, 0))

    def per_b(rows):
        return pl.BlockSpec((None, rows, LANES), lambda b, s, pt: (b, 0, 0))

    row = pl.BlockSpec((1, LANES), lambda b, s, pt: (0, 0))
    blocks = [((PH, LANES), F32)] * (2 * G) + [((3, R, PH), F32)]
    grid_spec = pltpu.PrefetchScalarGridSpec(
        num_scalar_prefetch=1,
        grid=(DB, n_pages // G),
        in_specs=[per_b(R), per_b(H_DIFF), per_b(H_DIFF),
                  pl.BlockSpec((3, R, PH), lambda b, s, pt: (0, 0, 0)),
                  pl.BlockSpec((SUBLANES, LANES), lambda b, s, pt: (0, 0)), row]
        + [pspec(i) for i in range(G)] + [pspec(i) for i in range(G)],
        out_specs=per_b(H_DIFF),
        scratch_shapes=[pltpu.VMEM((R, 1), F32), pltpu.VMEM((R, 1), F32), pltpu.VMEM((R, LANES), F32)],
    )
    return pl.pallas_call(
        functools.partial(_decode_kernel, G=G, lam_init=lam_init),
        grid_spec=grid_spec,
        out_shape=jax.ShapeDtypeStruct((DB, H_DIFF, LANES), F32),
        compiler_params=pltpu.CompilerParams(
            dimension_semantics=("parallel", "arbitrary"), vmem_limit_bytes=_vmem_limit(blocks)),
        name=name,
    )(page_table, qt, knew, vnew, dec_bias, lam_rows, gain_df.reshape(1, LANES),
      *([pages_k] * G), *([pages_v] * G))


def _pad_rows(x, rows):
    return jnp.pad(x, ((0, rows - x.shape[0]), (0, 0)))


def kernel(x_prompt, x_sample, cache_k, cache_v, state_delta, state_conv, page_table, rel_bias, norm_mix, w_in, conv_w, a_log, dt_bias, norm_dn, w_o_dn, lam_q1, lam_k1, lam_q2, lam_k2, norm_diff, w_o_diff, w_out, norm_mlp, w_up, w_down, norm_final):
    B, T, D = x_prompt.shape
    DB = x_sample.shape[0]
    depth = w_in.shape[0]
    n_pool, page = cache_k.shape[1], cache_k.shape[2]
    conv_dim = conv_w.shape[-1]
    n_ba = 2 * H_DN
    assert conv_dim == 3 * H_DN * DN_DK and x_sample.shape[1] == 1
    c_gate = conv_dim + n_ba
    c_kdf = c_gate + H_DN * DN_DV + H_DIFF * 2 * DIFF_DQ
    c_gbr = c_kdf + H_DIFF * 2 * DIFF_DQ + H_DIFF * DIFF_DV
    f_widths = (H_DIFF * 2 * DIFF_DQ, H_DIFF * DIFF_DV, LANES)
    MS = 16
    tq = min(512, T)
    n_tail = B * (CONV_W - 1)
    assert n_tail <= MS

    xp = x_prompt.reshape(B * T, D)
    xs = _pad_rows(x_sample.reshape(DB, D), MS)
    diag, sub = _bias_tiles(rel_bias, tq, tq)
    dec_bias = _dec_bias(rel_bias, page)
    pages_k = cache_k.reshape(depth, n_pool, page * H_DIFF, 2 * DIFF_DQ)
    pages_v = cache_v.reshape(depth, n_pool, page * H_DIFF, DIFF_DV)
    in_map0 = jnp.arange(2 * DIFF_DQ) < DIFF_DQ

    kp_l, vp_l, sp_l, cp_l, ks_l, vs_l, ss_l, cs_l = [], [], [], [], [], [], [], []
    for l in range(depth):
        lam_init = 0.8 - 0.6 * math.exp(-0.3 * l)
        wl = w_in[l]
        w_b = jnp.concatenate([wl[:, :conv_dim], wl[:, c_gate:c_kdf], wl[:, c_gbr:]], axis=1).astype(BF16)
        w_f = jnp.pad(jnp.concatenate([wl[:, c_kdf:c_gbr], wl[:, conv_dim:c_gate]], axis=1),
                      ((0, 0), (0, LANES - n_ba))).astype(BF16)
        w_odn, w_odf, w_o = w_o_dn[l].astype(BF16), w_o_diff[l].astype(BF16), w_out[l].astype(BF16)
        w_u, w_d = w_up[l].astype(BF16), w_down[l].astype(BF16)
        alog_row = jnp.pad(a_log[l], (H_DN, LANES - n_ba)).reshape(1, LANES)
        dtb_row = jnp.pad(dt_bias[l], (H_DN, LANES - n_ba)).reshape(1, LANES)
        lam_rows = jnp.pad(jnp.stack([lam_q1[l], lam_k1[l], lam_q2[l], lam_k2[l]]),
                           ((0, SUBLANES - 4), (0, LANES - DIFF_DQ)))
        last = l == depth - 1

        x_tail = _pad_rows(xp.reshape(B, T, D)[:, T - (CONV_W - 1):].reshape(n_tail, D), MS)
        zst = _mm(jnp.concatenate([xs, x_tail], axis=0), w_b, gain=norm_mix[l], tn=1024, name=f"in_proj_s{l}")
        zs, z_tail = zst[:MS], zst[MS:]
        zb = _mm(xp, w_b, gain=norm_mix[l], out_dtype=BF16, tn=1536, name=f"in_proj_p{l}")
        kf, vf, ba = _mm_split(xp, w_f, norm_mix[l], f_widths, name=f"in_proj_kv_p{l}")
        o_dn, s_fin = _delta_prompt(zb, ba, conv_w[l], alog_row, dtb_row, norm_dn[l], B, T, name=f"delta_p{l}")
        o_df = _attn_prompt(zb, kf, vf, diag, sub, lam_rows, norm_diff[l], B, T, lam_init, tq=tq,
                            name=f"attn_p{l}")
        merged = _merge(o_dn, o_df, zb, w_odn, w_odf, name=f"merge_p{l}")
        xp = _mm(merged, w_o, residual=xp, tn=1024, name=f"out_proj_p{l}")
        xp = _mlp(xp, norm_mlp[l], w_u, w_d, final_gain=norm_final if last else None, name=f"mlp_p{l}")
        kp_l.append(kf.reshape(B, T, H_DIFF, 2 * DIFF_DQ))
        vp_l.append(vf.reshape(B, T, H_DIFF, DIFF_DV))
        sp_l.append(s_fin)
        cp_l.append(z_tail[:n_tail, :conv_dim].reshape(B, CONV_W - 1, conv_dim))

        ks16, vs16, bas16 = _mm_split(xs, w_f, norm_mix[l], f_widths, name=f"in_proj_kv_s{l}")
        zs3 = zs[:DB].reshape(DB, 1, -1)
        o_dn_s, s_new = _delta_step(zs3, bas16[:DB].reshape(DB, 1, LANES), state_conv,
                                    conv_w[l], alog_row, dtb_row, norm_dn[l], state_delta, l, name=f"delta_s{l}")
        q_s = zs[:DB, _C_QDF:_C_QDF + 1024].reshape(DB, H_DIFF, 2 * DIFF_DQ) * (DIFF_DQ ** -0.5)
        qt = jnp.concatenate([jnp.where(in_map0, q_s, 0.0), jnp.where(in_map0, 0.0, q_s)], axis=1).astype(BF16)
        k_s = ks16[:DB].reshape(DB, 1, -1)
        v_s = vs16[:DB].reshape(DB, 1, -1)
        o_df_s = _decode_attn(qt, k_s.reshape(DB, H_DIFF, 2 * DIFF_DQ), v_s.reshape(DB, H_DIFF, DIFF_DV),
                              dec_bias, lam_rows, norm_diff[l], pages_k, pages_v, page_table, l,
                              lam_init, name=f"attn_s{l}")
        merged_s = _merge(_pad_rows(o_dn_s.reshape(DB, -1), MS).astype(BF16),
                          _pad_rows(o_df_s.reshape(DB, -1), MS).astype(BF16), zs,
                          w_odn, w_odf, tn=1024, name=f"merge_s{l}")
        xs = _mm(merged_s, w_o, residual=xs, tn=1024, name=f"out_proj_s{l}")
        xs = _mlp(xs, norm_mlp[l], w_u, w_d, final_gain=norm_final if last else None, name=f"mlp_s{l}")
        ks_l.append(k_s.reshape(DB, 1, H_DIFF, 2 * DIFF_DQ))
        vs_l.append(v_s.reshape(DB, 1, H_DIFF, DIFF_DV))
        ss_l.append(s_new)
        cs_l.append(jnp.concatenate([state_conv[l][:, 1:], zs3[:, :, :conv_dim]], axis=1))

    y_prompt = xp.reshape(B, T, D)
    y_sample = xs[:DB].reshape(DB, 1, D)
    return (y_prompt, y_sample,
            jnp.stack(kp_l), jnp.stack(vp_l), jnp.stack(sp_l), jnp.stack(cp_l),
            jnp.stack(ks_l), jnp.stack(vs_l), jnp.stack(ss_l), jnp.stack(cs_l))
```

```python
import functools
import math

import jax
import jax.numpy as jnp
from jax import lax
from jax.experimental import pallas as pl
from jax.experimental.pallas import tpu as pltpu

F32 = jnp.float32
BF16 = jnp.bfloat16

H_DN = 8
DN_DK = 128
DN_DV = 128
CONV_W = 4
H_DIFF = 8
DIFF_DQ = 64
DIFF_DV = 128
NUM_BUCKETS = 32
MAX_DISTANCE = 128
EPS = 1e-6

LANES = 128
SUBLANES = 8
VMEM_BYTES_V7X = 64 * 1024 * 1024
VMEM_INTERNAL_ALLOWANCE = 12 * 1024 * 1024

_MAX_EXACT = NUM_BUCKETS // 2
_BUCKET_START = tuple(
    k if k <= _MAX_EXACT else math.ceil(
        _MAX_EXACT * (MAX_DISTANCE / _MAX_EXACT) ** ((k - _MAX_EXACT) / (NUM_BUCKETS - _MAX_EXACT)))
    for k in range(NUM_BUCKETS))
_FAR_REL = _BUCKET_START[-1]

_C_QDN, _C_KDN, _C_VDN, _C_GATE = 0, 1024, 2048, 3072
_C_QDF = 4096
_C_GA = 5120


def _nbytes(shape, dtype):
    return math.prod(shape) * jnp.dtype(dtype).itemsize


def _vmem_limit(pipelined, resident=()):
    need = 2 * sum(_nbytes(s, d) for s, d in pipelined) + sum(_nbytes(s, d) for s, d in resident)
    return int(min(need + VMEM_INTERNAL_ALLOWANCE, VMEM_BYTES_V7X - 2 * 1024 * 1024))


def _dot(a, b):
    return jnp.dot(a.astype(BF16), b.astype(BF16), preferred_element_type=F32)


def _dot_nt(a, b):
    return lax.dot_general(a.astype(BF16), b.astype(BF16), (((1,), (1,)), ((), ())),
                           preferred_element_type=F32)


def _dot_tn(a, b):
    return lax.dot_general(a.astype(BF16), b.astype(BF16), (((0,), (0,)), ((), ())),
                           preferred_element_type=F32)


def _bdot(a, b):
    return lax.dot_general(a.astype(BF16), b.astype(BF16), (((2,), (1,)), ((0,), (0,))),
                           preferred_element_type=F32)


def _bdot_nt(a, b):
    return lax.dot_general(a.astype(BF16), b.astype(BF16), (((2,), (2,)), ((0,), (0,))),
                           preferred_element_type=F32)


def _bdot_tn(a, b):
    return lax.dot_general(a.astype(BF16), b.astype(BF16), (((1,), (1,)), ((0,), (0,))),
                           preferred_element_type=F32)


def _rms(x, gain):
    return x * lax.rsqrt(jnp.mean(x * x, axis=-1, keepdims=True) + EPS) * gain


def _silu(x):
    h = 0.5 * x
    return h + h * jnp.tanh(h)


def _softplus(x):
    return jnp.maximum(x, 0.0) + jnp.log1p(jnp.exp(-jnp.abs(x)))


def _norm_rows_to(x_ref, g_ref, h_ref):
    tm = x_ref.shape[0]
    rows = min(tm, 128)

    def body(i, c):
        sl = pl.ds(pl.multiple_of(i * rows, rows), rows)
        h_ref[sl, :] = _rms(x_ref[sl, :], g_ref[...]).astype(h_ref.dtype)
        return c

    lax.fori_loop(0, tm // rows, body, 0)


def _mm_kernel(*refs, has_gain, has_res):
    it = iter(refs)
    x_ref = next(it)
    g_ref = next(it) if has_gain else None
    w_ref = next(it)
    r_ref = next(it) if has_res else None
    o_ref = next(it)
    if has_gain:
        h_ref = next(it)

        @pl.when(pl.program_id(1) == 0)
        def _():
            _norm_rows_to(x_ref, g_ref, h_ref)

        lhs = h_ref[...]
    else:
        lhs = x_ref[...]
    acc = jnp.dot(lhs, w_ref[...], preferred_element_type=F32)
    if has_res:
        acc = acc + r_ref[...]
    o_ref[...] = acc.astype(o_ref.dtype)


def _mm(x, w, *, gain=None, residual=None, out_dtype=F32, tm=1024, tn=512, name):
    M, K = x.shape
    N = w.shape[1]
    tm, tn = min(tm, M), min(tn, N)
    assert M % tm == 0 and N % tn == 0, (M, N, tm, tn)
    in_specs = [pl.BlockSpec((tm, K), lambda i, j: (i, 0))]
    args = [x]
    blocks = [((tm, K), x.dtype), ((K, tn), w.dtype), ((tm, tn), out_dtype)]
    scratch = []
    if gain is not None:
        in_specs.append(pl.BlockSpec((1, K), lambda i, j: (0, 0)))
        args.append(gain.reshape(1, K))
        scratch.append(pltpu.VMEM((tm, K), BF16))
    in_specs.append(pl.BlockSpec((K, tn), lambda i, j: (0, j)))
    args.append(w)
    if residual is not None:
        in_specs.append(pl.BlockSpec((tm, tn), lambda i, j: (i, j)))
        args.append(residual)
        blocks.append(((tm, tn), residual.dtype))
    return pl.pallas_call(
        functools.partial(_mm_kernel, has_gain=gain is not None, has_res=residual is not None),
        grid=(M // tm, N // tn),
        in_specs=in_specs,
        out_specs=pl.BlockSpec((tm, tn), lambda i, j: (i, j)),
        out_shape=jax.ShapeDtypeStruct((M, N), out_dtype),
        scratch_shapes=scratch,
        compiler_params=pltpu.CompilerParams(
            dimension_semantics=("parallel", "arbitrary"),
            vmem_limit_bytes=_vmem_limit(blocks, [((tm, K), BF16)] if gain is not None else [])),
        name=name,
    )(*args)


def _mm_split_kernel(x_ref, g_ref, w_ref, *rest, widths):
    out_refs, h_ref = rest[:-1], rest[-1]
    _norm_rows_to(x_ref, g_ref, h_ref)
    acc = jnp.dot(h_ref[...], w_ref[...], preferred_element_type=F32)
    col = 0
    for o_ref, width in zip(out_refs, widths):
        o_ref[...] = acc[:, col:col + width]
        col += width


def _mm_split(x, w, gain, widths, *, tm=512, name):
    M, K = x.shape
    N = w.shape[1]
    tm = min(tm, M)
    assert M % tm == 0 and sum(widths) == N and all(wd % LANES == 0 for wd in widths)
    blocks = [((tm, K), x.dtype), ((K, N), w.dtype), ((tm, N), F32)]
    return pl.pallas_call(
        functools.partial(_mm_split_kernel, widths=tuple(widths)),
        grid=(M // tm,),
        in_specs=[pl.BlockSpec((tm, K), lambda i: (i, 0)), pl.BlockSpec((1, K), lambda i: (0, 0)),
                  pl.BlockSpec((K, N), lambda i: (0, 0))],
        out_specs=[pl.BlockSpec((tm, wd), lambda i: (i, 0)) for wd in widths],
        out_shape=[jax.ShapeDtypeStruct((M, wd), F32) for wd in widths],
        scratch_shapes=[pltpu.VMEM((tm, K), BF16)],
        compiler_params=pltpu.CompilerParams(
            dimension_semantics=("parallel",),
            vmem_limit_bytes=_vmem_limit(blocks, [((tm, K), BF16), ((tm, N), F32)])),
        name=name,
    )(x, gain.reshape(1, K), w)


def _merge_kernel(odn_ref, odf_ref, ga_ref, gb_ref, wdn_ref, wdf_ref, o_ref):
    a = jnp.dot(odn_ref[...], wdn_ref[...], preferred_element_type=F32)
    b = jnp.dot(odf_ref[...], wdf_ref[...], preferred_element_type=F32)
    ga = jax.nn.sigmoid(ga_ref[...].astype(F32))
    gb = jax.nn.sigmoid(gb_ref[...].astype(F32))
    o_ref[...] = (ga * a + gb * b).astype(o_ref.dtype)


def _merge(o_dn, o_df, z, w_dn, w_df, *, tm=1024, tn=1024, name):
    M, K = o_dn.shape
    N = w_dn.shape[1]
    tm, tn = min(tm, M), min(tn, N)
    assert M % tm == 0 and N % tn == 0 and _C_GA % tn == 0
    ga0 = _C_GA // tn
    gb0 = (_C_GA + N) // tn
    blocks = [((tm, K), BF16)] * 2 + [((tm, tn), z.dtype)] * 2 + [((K, tn), BF16)] * 2 + [((tm, tn), BF16)]
    return pl.pallas_call(
        _merge_kernel,
        grid=(M // tm, N // tn),
        in_specs=[
            pl.BlockSpec((tm, K), lambda i, j: (i, 0)),
            pl.BlockSpec((tm, K), lambda i, j: (i, 0)),
            pl.BlockSpec((tm, tn), lambda i, j: (i, ga0 + j)),
            pl.BlockSpec((tm, tn), lambda i, j: (i, gb0 + j)),
            pl.BlockSpec((K, tn), lambda i, j: (0, j)),
            pl.BlockSpec((K, tn), lambda i, j: (0, j)),
        ],
        out_specs=pl.BlockSpec((tm, tn), lambda i, j: (i, j)),
        out_shape=jax.ShapeDtypeStruct((M, N), BF16),
        compiler_params=pltpu.CompilerParams(
            dimension_semantics=("parallel", "arbitrary"), vmem_limit_bytes=_vmem_limit(blocks)),
        name=name,
    )(o_dn, o_df, z, z, w_dn, w_df)


def _mlp_kernel(*refs, has_final):
    if has_final:
        x_ref, g_ref, wu_ref, wd_ref, gf_ref, o_ref, h_ref, acc_ref = refs
    else:
        x_ref, g_ref, wu_ref, wd_ref, o_ref, h_ref, acc_ref = refs
        gf_ref = None
    _mlp_step(x_ref, g_ref, wu_ref, wd_ref, gf_ref, o_ref, h_ref, acc_ref)


def _mlp_step(x_ref, g_ref, wu_ref, wd_ref, gf_ref, o_ref, h_ref, acc_ref):
    has_final = gf_ref is not None
    f = pl.program_id(1)

    @pl.when(f == 0)
    def _():
        _norm_rows_to(x_ref, g_ref, h_ref)
        acc_ref[...] = jnp.zeros_like(acc_ref)

    u = jnp.dot(h_ref[...], wu_ref[...], preferred_element_type=F32)
    u = jnp.square(jnp.maximum(u, 0.0)).astype(BF16)
    acc_ref[...] += jnp.dot(u, wd_ref[...], preferred_element_type=F32)

    @pl.when(f == pl.num_programs(1) - 1)
    def _():
        y = x_ref[...] + acc_ref[...]
        if has_final:
            y = _rms(y, gf_ref[...])
        o_ref[...] = y


def _mlp(x, gain, w_up, w_down, *, final_gain=None, tm=512, tf=1024, name):
    M, D = x.shape
    FF = w_up.shape[1]
    tm, tf = min(tm, M), min(tf, FF)
    assert M % tm == 0 and FF % tf == 0
    in_specs = [
        pl.BlockSpec((tm, D), lambda i, f: (i, 0)),
        pl.BlockSpec((1, D), lambda i, f: (0, 0)),
        pl.BlockSpec((D, tf), lambda i, f: (0, f)),
        pl.BlockSpec((tf, D), lambda i, f: (f, 0)),
    ]
    args = [x, gain.reshape(1, D), w_up, w_down]
    if final_gain is not None:
        in_specs.append(pl.BlockSpec((1, D), lambda i, f: (0, 0)))
        args.append(final_gain.reshape(1, D))
    blocks = [((tm, D), F32), ((D, tf), BF16), ((tf, D), BF16), ((tm, D), F32)]
    resident = [((tm, D), BF16), ((tm, D), F32)]
    return pl.pallas_call(
        functools.partial(_mlp_kernel, has_final=final_gain is not None),
        grid=(M // tm, FF // tf),
        in_specs=in_specs,
        out_specs=pl.BlockSpec((tm, D), lambda i, f: (i, 0)),
        out_shape=jax.ShapeDtypeStruct((M, D), F32),
        scratch_shapes=[pltpu.VMEM((tm, D), BF16), pltpu.VMEM((tm, D), F32)],
        compiler_params=pltpu.CompilerParams(
            dimension_semantics=("parallel", "arbitrary"), vmem_limit_bytes=_vmem_limit(blocks, resident)),
        name=name,
    )(*args)


def _bias_of_rel(rb_ref, h, rel):
    far = rb_ref[NUM_BUCKETS - 1, h]
    v = jnp.full(rel.shape, rb_ref[0, h] - far, F32)
    for k in range(1, NUM_BUCKETS):
        v = jnp.where(rel >= _BUCKET_START[k], rb_ref[k, h] - far, v)
    return v


def _bias_tiles_kernel(rb_ref, diag_ref, sub_ref, *, tk):
    h = pl.program_id(0)
    shape = diag_ref.shape
    rel = lax.broadcasted_iota(jnp.int32, shape, 0) - lax.broadcasted_iota(jnp.int32, shape, 1)
    diag_ref[...] = jnp.where(rel >= 0, _bias_of_rel(rb_ref, h, rel), -jnp.inf)
    sub_ref[...] = _bias_of_rel(rb_ref, h, rel + tk)


def _bias_tiles(rel_bias, tq, tk):
    assert tq == tk and tk + 1 >= _FAR_REL
    H = rel_bias.shape[1]
    return pl.pallas_call(
        functools.partial(_bias_tiles_kernel, tk=tk),
        grid=(H,),
        in_specs=[pl.BlockSpec(memory_space=pltpu.SMEM)],
        out_specs=[pl.BlockSpec((None, tq, tk), lambda h: (h, 0, 0))] * 2,
        out_shape=[jax.ShapeDtypeStruct((H, tq, tk), F32)] * 2,
        name="t5_bias_tiles",
    )(rel_bias)


def _dec_bias_kernel(rb_ref, o_ref, *, page):
    shape = o_ref.shape[1:]
    row_head = lax.broadcasted_iota(jnp.int32, shape, 0) & (H_DIFF - 1)
    lane = lax.broadcasted_iota(jnp.int32, shape, 1)
    own_head = (lane & (H_DIFF - 1)) == row_head
    rel = page - (lane >> (H_DIFF.bit_length() - 1))
    last = jnp.zeros(shape, F32)
    new = jnp.zeros(shape, F32)
    for h in range(H_DIFF):
        sel = row_head == h
        last = jnp.where(sel, _bias_of_rel(rb_ref, h, rel), last)
        new = jnp.where(sel, rb_ref[0, h] - rb_ref[NUM_BUCKETS - 1, h], new)
    o_ref[0] = jnp.where(own_head, 0.0, -jnp.inf)
    o_ref[1] = jnp.where(own_head, last, -jnp.inf)
    o_ref[2] = new


def _dec_bias(rel_bias, page):
    assert page + 1 >= _FAR_REL and H_DIFF & (H_DIFF - 1) == 0
    return pl.pallas_call(
        functools.partial(_dec_bias_kernel, page=page),
        in_specs=[pl.BlockSpec(memory_space=pltpu.SMEM)],
        out_specs=pl.BlockSpec(memory_space=pltpu.VMEM),
        out_shape=jax.ShapeDtypeStruct((3, 2 * H_DIFF, page * H_DIFF), F32),
        name="t5_bias_decode",
    )(rel_bias)


def _unit_lower_inverse(L, ri, ci):
    C = L.shape[-1]
    eye = (ri == ci).astype(F32)
    T = None
    lb = 0
    while (1 << lb) < C:
        below = ((ri >> (lb + 1)) == (ci >> (lb + 1))) & (((ri >> lb) & 1) == 1) & (((ci >> lb) & 1) == 0)
        Lb = jnp.where(below, L, 0.0)
        T = eye - Lb if T is None else T - _bdot(_bdot(T, Lb), T)
        lb += 1
    return T


def _delta_chunk_local(q, k, v, beta, g):
    N, C, _ = q.shape
    ri = lax.broadcasted_iota(jnp.int32, (N, C, C), 1)
    ci = lax.broadcasted_iota(jnp.int32, (N, C, C), 2)
    tril = ri >= ci
    g_cols = jnp.broadcast_to(g, (N, C, C))
    g_row = jnp.sum(jnp.where(ri == ci, g_cols, 0.0), axis=1, keepdims=True)
    cum_col = jnp.sum(jnp.where(tril, jnp.broadcast_to(g_row, (N, C, C)), 0.0), axis=2, keepdims=True)
    cum_row = jnp.sum(jnp.where(ri <= ci, g_cols, 0.0), axis=1, keepdims=True)
    decay = jnp.where(tril, jnp.exp(jnp.where(tril, cum_col - cum_row, 0.0)), 0.0)
    g_last = jnp.sum(g, axis=1, keepdims=True)
    e_col = jnp.exp(cum_col)
    kb = k * beta
    vb = v * beta
    L = jnp.where(ri > ci, _bdot_nt(kb, k) * decay, 0.0)
    T = _unit_lower_inverse(L, ri, ci)
    vk = _bdot(T, jnp.concatenate([vb, kb * e_col], axis=2))
    qk = _bdot_nt(q, k) * decay
    return (vk[:, :, :DN_DV], vk[:, :, DN_DV:], q * e_col, qk, k * jnp.exp(g_last - cum_col),
            jnp.broadcast_to(jnp.exp(g_last), (N, SUBLANES, LANES)))


def _delta_kernel(zq_ref, zk_ref, zv_ref, zg_ref, ba_ref, wq_ref, wk_ref, wv_ref, alog_ref, dtb_ref,
                  gain_ref, o_ref, sout_ref, xq_s, xk_s, xv_s, q_s, k_s, v_s, S_s,
                  val_s, kcum_s, qe_s, qk_s, kdec_s, egl_s, *, HG, TT, C):
    hgi = pl.program_id(1)
    t = pl.program_id(2)
    W = HG * LANES
    NC = TT // C

    @pl.when(t == 0)
    def _():
        for x_s in (xq_s, xk_s, xv_s):
            x_s[0:SUBLANES, :] = jnp.zeros((SUBLANES, W), F32)
        S_s[...] = jnp.zeros_like(S_s)

    for z_ref, x_s, w_ref, dst in ((zq_ref, xq_s, wq_ref, q_s), (zk_ref, xk_s, wk_ref, k_s),
                                   (zv_ref, xv_s, wv_ref, v_s)):
        x_s[SUBLANES:SUBLANES + TT, :] = z_ref[...].astype(F32)
        xfull = x_s[...]
        acc = xfull[SUBLANES:, :] * w_ref[CONV_W - 1:CONV_W, :]
        for j in range(CONV_W - 1):
            shifted = pltpu.roll(xfull, CONV_W - 1 - j, axis=0)[SUBLANES:, :]
            acc = acc + shifted * w_ref[j:j + 1, :]
        dst[...] = _silu(acc)
        x_s[0:SUBLANES, :] = x_s[TT:TT + SUBLANES, :]

    ba = ba_ref[...]
    bt = jax.nn.sigmoid(ba)
    gt = -jnp.exp(alog_ref[...]) * _softplus(ba + dtb_ref[...])
    lane = lax.broadcasted_iota(jnp.int32, (TT, LANES), 1)
    qs, ks, vs, betas, gs = [], [], [], [], []
    for hh in range(HG):
        lanes = slice(hh * LANES, (hh + 1) * LANES)
        head = hgi * HG + hh
        qh = q_s[:, lanes]
        qh = qh * lax.rsqrt(jnp.sum(qh * qh, axis=-1, keepdims=True) + EPS) * (DN_DK ** -0.5)
        kh = k_s[:, lanes]
        kh = kh * lax.rsqrt(jnp.sum(kh * kh, axis=-1, keepdims=True) + EPS)
        qs.append(qh.reshape(NC, C, LANES))
        ks.append(kh.reshape(NC, C, LANES))
        vs.append(v_s[:, lanes].reshape(NC, C, LANES))
        betas.append(jnp.sum(jnp.where(lane == head, bt, 0.0), axis=-1, keepdims=True).reshape(NC, C, 1))
        gs.append(jnp.sum(jnp.where(lane == head + H_DN, gt, 0.0), axis=-1, keepdims=True).reshape(NC, C, 1))

    cat = functools.partial(jnp.concatenate, axis=0)
    local = _delta_chunk_local(cat(qs), cat(ks), cat(vs), cat(betas), cat(gs))
    for ref, val in zip((val_s, kcum_s, qe_s, qk_s, kdec_s, egl_s), local):
        ref[...] = val.reshape(ref.shape)

    for c in range(NC):
        rows = slice(c * C, (c + 1) * C)
        S = S_s[...]
        r = _bdot(jnp.concatenate([kcum_s[:, c], qe_s[:, c]], axis=1), S)
        v_new = val_s[:, c] - r[:, :C]
        o = r[:, C:] + _bdot(qk_s[:, c], v_new)
        S_s[...] = S * egl_s[:, c, 0:1, 0:1] + _bdot_tn(kdec_s[:, c], v_new)
        for hh in range(HG):
            lanes = slice(hh * LANES, (hh + 1) * LANES)
            gate = _silu(zg_ref[rows, lanes].astype(F32))
            o_ref[rows, lanes] = (_rms(o[hh], gain_ref[...]) * gate).astype(o_ref.dtype)

    @pl.when(t == pl.num_programs(2) - 1)
    def _():
        sout_ref[...] = S_s[...]


def _delta_prompt(zb, ba, conv_w, alog_row, dtb_row, gain_dn, B, T, *, HG=8, TT=256, C=64, name):
    M = zb.shape[0]
    TT = min(TT, T)
    C = min(C, TT)
    assert T % TT == 0 and TT % C == 0 and H_DN % HG == 0 and C & (C - 1) == 0 and C % 16 == 0
    NT = T // TT
    NC = TT // C
    W = HG * LANES
    nhg = H_DN // HG

    def zspec(col0):
        return pl.BlockSpec((TT, W), lambda b, hg, t: (b * NT + t, col0 // W + hg))

    def wspec(col0):
        return pl.BlockSpec((CONV_W, W), lambda b, hg, t: (0, col0 // W + hg))

    row = pl.BlockSpec((1, LANES), lambda b, hg, t: (0, 0))
    blocks = [((TT, W), zb.dtype)] * 4 + [((TT, LANES), F32)] + [((TT, W), BF16), ((HG, DN_DK, DN_DV), F32)]
    scratch = ([((TT + SUBLANES, W), F32)] * 3 + [((TT, W), F32)] * 3 + [((HG, DN_DK, DN_DV), F32)]
               + [((HG, NC, C, LANES), F32)] * 3 + [((HG, NC, C, C), F32)] + [((HG, NC, C, LANES), F32)]
               + [((HG, NC, SUBLANES, LANES), F32)])
    temporaries = [((HG * NC, C, LANES), F32)] * 12
    return pl.pallas_call(
        functools.partial(_delta_kernel, HG=HG, TT=TT, C=C),
        grid=(B, nhg, NT),
        in_specs=[zspec(_C_QDN), zspec(_C_KDN), zspec(_C_VDN), zspec(_C_GATE),
                  pl.BlockSpec((TT, LANES), lambda b, hg, t: (b * NT + t, 0)),
                  wspec(0), wspec(H_DN * DN_DK), wspec(2 * H_DN * DN_DK), row, row, row],
        out_specs=[pl.BlockSpec((TT, W), lambda b, hg, t: (b * NT + t, hg)),
                   pl.BlockSpec((None, HG, DN_DK, DN_DV), lambda b, hg, t: (b, hg, 0, 0))],
        out_shape=[jax.ShapeDtypeStruct((M, H_DN * DN_DV), BF16),
                   jax.ShapeDtypeStruct((B, H_DN, DN_DK, DN_DV), F32)],
        scratch_shapes=[pltpu.VMEM(shape, dtype) for shape, dtype in scratch],
        compiler_params=pltpu.CompilerParams(
            dimension_semantics=("parallel", "parallel", "arbitrary"),
            vmem_limit_bytes=_vmem_limit(blocks, scratch + temporaries)),
        name=name,
    )(zb, zb, zb, zb, ba, conv_w, conv_w, conv_w, alog_row, dtb_row, gain_dn.reshape(1, LANES))


def _lambda(lam_ref, lam_init):
    lv = lam_ref[...]
    s1 = jnp.sum(lv[0:1, :] * lv[1:2, :], axis=-1, keepdims=True)
    s2 = jnp.sum(lv[2:3, :] * lv[3:4, :], axis=-1, keepdims=True)
    return jnp.exp(s1) - jnp.exp(s2) + lam_init


def _attn_kernel(zq_ref, zk_ref, zv_ref, diag_ref, sub_ref, lam_ref, gain_ref, o_ref, kb_s, vb_s, *, tq, lam_init):
    T = zk_ref.shape[0]
    kb_s[...] = zk_ref[...].astype(BF16)
    vb_s[...] = zv_ref[...].astype(BF16)
    lam = _lambda(lam_ref, lam_init)
    lane = lax.broadcasted_iota(jnp.int32, (tq, LANES), 1)

    def scores(qi, mi):
        q = zq_ref[qi * tq:(qi + 1) * tq, :].astype(F32) * (DIFF_DQ ** -0.5)
        in_map = (lane >= DIFF_DQ) if mi else (lane < DIFF_DQ)
        qm = jnp.where(in_map, q, 0.0).astype(BF16)
        s = lax.dot_general(qm, kb_s[0:(qi + 1) * tq, :], (((1,), (1,)), ((), ())), preferred_element_type=F32)
        parts = []
        if qi >= 2:
            parts.append(s[:, :(qi - 1) * tq])
        if qi >= 1:
            parts.append(s[:, (qi - 1) * tq:qi * tq] + sub_ref[...])
        parts.append(s[:, qi * tq:] + diag_ref[...])
        return parts[0] if len(parts) == 1 else jnp.concatenate(parts, axis=1)

    items = [(qi, mi) for qi in range(T // tq) for mi in range(2)]
    s_next = scores(*items[0])
    outs = []
    for idx, (qi, mi) in enumerate(items):
        s = s_next
        if idx + 1 < len(items):
            s_next = scores(*items[idx + 1])
        p = jnp.exp(s - jnp.max(s, axis=-1, keepdims=True))
        l = jnp.sum(p, axis=-1, keepdims=True)
        outs.append(jnp.dot(p.astype(BF16), vb_s[0:(qi + 1) * tq, :], preferred_element_type=F32) / l)
        if mi == 1:
            o = outs[0] - lam * outs[1]
            outs = []
            o_ref[qi * tq:(qi + 1) * tq, :] = (_rms(o, gain_ref[...]) * (1.0 - lam_init)).astype(o_ref.dtype)


def _attn_prompt(zb, kf, vf, diag, sub, lam_rows, gain_df, B, T, lam_init, *, tq, name):
    M = zb.shape[0]
    assert T % tq == 0
    row = pl.BlockSpec((1, LANES), lambda b, h: (0, 0))
    blocks = [((T, LANES), zb.dtype)] + [((T, LANES), F32)] * 2 + [((tq, tq), F32)] * 2 + [((T, LANES), BF16)]
    resident = [((T, LANES), BF16)] * 2 + [((tq, T), F32)] * 4
    return pl.pallas_call(
        functools.partial(_attn_kernel, tq=tq, lam_init=lam_init),
        grid=(B, H_DIFF),
        in_specs=[pl.BlockSpec((T, LANES), lambda b, h: (b, _C_QDF // LANES + h)),
                  pl.BlockSpec((T, LANES), lambda b, h: (b, h)),
                  pl.BlockSpec((T, LANES), lambda b, h: (b, h)),
                  pl.BlockSpec((None, tq, tq), lambda b, h: (h, 0, 0)),
                  pl.BlockSpec((None, tq, tq), lambda b, h: (h, 0, 0)),
                  pl.BlockSpec((SUBLANES, LANES), lambda b, h: (0, 0)), row],
        out_specs=pl.BlockSpec((T, LANES), lambda b, h: (b, h)),
        out_shape=jax.ShapeDtypeStruct((M, H_DIFF * DIFF_DV), BF16),
        scratch_shapes=[pltpu.VMEM((T, LANES), BF16)] * 2,
        compiler_params=pltpu.CompilerParams(
            dimension_semantics=("parallel", "parallel"),
            vmem_limit_bytes=_vmem_limit(blocks, resident)),
        name=name,
    )(zb, kf, vf, diag, sub, lam_rows, gain_df.reshape(1, LANES))


def _delta_step_kernel(zq_ref, zk_ref, zv_ref, zg_ref, ba_ref, cq_ref, ck_ref, cv_ref, wq_ref, wk_ref, wv_ref,
                       alog_ref, dtb_ref, gain_ref, s_ref, o_ref, sout_ref):
    def conv(z_ref, c_ref, w_ref):
        acc = z_ref[...] * w_ref[CONV_W - 1:CONV_W, :]
        for j in range(CONV_W - 1):
            acc = acc + c_ref[j:j + 1, :] * w_ref[j:j + 1, :]
        return _silu(acc)

    cq, ck, cv = conv(zq_ref, cq_ref, wq_ref), conv(zk_ref, ck_ref, wk_ref), conv(zv_ref, cv_ref, wv_ref)
    ba = ba_ref[...]
    bt = jax.nn.sigmoid(ba)
    gt = -jnp.exp(alog_ref[...]) * _softplus(ba + dtb_ref[...])
    lane = lax.broadcasted_iota(jnp.int32, (1, LANES), 1)
    pad = jnp.zeros((SUBLANES - 2, LANES), F32)
    for h in range(H_DN):
        lanes = slice(h * LANES, (h + 1) * LANES)
        q = cq[:, lanes]
        q = q * lax.rsqrt(jnp.sum(q * q, axis=-1, keepdims=True) + EPS) * (DN_DK ** -0.5)
        k = ck[:, lanes]
        k = k * lax.rsqrt(jnp.sum(k * k, axis=-1, keepdims=True) + EPS)
        v = cv[:, lanes]
        beta = jnp.sum(jnp.where(lane == h, bt, 0.0), axis=-1, keepdims=True)
        eg = jnp.exp(jnp.sum(jnp.where(lane == h + H_DN, gt, 0.0), axis=-1, keepdims=True))
        S = s_ref[h]
        r = _dot(jnp.concatenate([k, q, pad], axis=0), S)
        v_new = beta * (v - eg * r[0:1])
        o = eg * r[1:2] + jnp.sum(q * k, axis=-1, keepdims=True) * v_new
        zeros7 = jnp.zeros((SUBLANES - 1, LANES), F32)
        outer = _dot_tn(jnp.concatenate([k, zeros7], axis=0), jnp.concatenate([v_new, zeros7], axis=0))
        sout_ref[h] = S * eg + outer
        o_ref[:, lanes] = (_rms(o, gain_ref[...]) * _silu(zg_ref[:, lanes])).astype(o_ref.dtype)


def _delta_step(z3, ba3, conv_state, conv_w, alog_row, dtb_row, gain_dn, state, layer, *, name):
    DB = z3.shape[0]
    W = H_DN * LANES

    def zspec(col0):
        return pl.BlockSpec((None, 1, W), lambda b: (b, 0, col0 // W))

    def cspec(col0):
        return pl.BlockSpec((None, None, CONV_W - 1, W), lambda b: (layer, b, 0, col0 // W))

    def wspec(col0):
        return pl.BlockSpec((CONV_W, W), lambda b: (0, col0 // W))

    row = pl.BlockSpec((1, LANES), lambda b: (0, 0))
    return pl.pallas_call(
        _delta_step_kernel,
        grid=(DB,),
        in_specs=[zspec(_C_QDN), zspec(_C_KDN), zspec(_C_VDN), zspec(_C_GATE),
                  pl.BlockSpec((None, 1, LANES), lambda b: (b, 0, 0)),
                  cspec(0), cspec(W), cspec(2 * W), wspec(0), wspec(W), wspec(2 * W), row, row, row,
                  pl.BlockSpec((None, None, H_DN, DN_DK, DN_DV), lambda b: (layer, b, 0, 0, 0))],
        out_specs=[pl.BlockSpec((None, 1, W), lambda b: (b, 0, 0)),
                   pl.BlockSpec((None, H_DN, DN_DK, DN_DV), lambda b: (b, 0, 0, 0))],
        out_shape=[jax.ShapeDtypeStruct((DB, 1, W), F32),
                   jax.ShapeDtypeStruct((DB, H_DN, DN_DK, DN_DV), F32)],
        compiler_params=pltpu.CompilerParams(dimension_semantics=("parallel",)),
        name=name,
    )(z3, z3, z3, z3, ba3, conv_state, conv_state, conv_state, conv_w, conv_w, conv_w,
      alog_row, dtb_row, gain_dn.reshape(1, LANES), state)


def _decode_kernel(pt_ref, qt_ref, knew_ref, vnew_ref, bias_ref, lam_ref, gain_ref, *rest, G, lam_init):
    m_s, l_s, acc_s = rest[2 * G + 1:]
    _decode_step(pl.program_id(1), pl.num_programs(1), qt_ref, knew_ref, vnew_ref, bias_ref, lam_ref, gain_ref,
                 rest[:G], rest[G:2 * G], rest[2 * G], m_s, l_s, acc_s, lam_init)


def _decode_step(step, nsteps, qt_ref, knew_ref, vnew_ref, bias_ref, lam_ref, gain_ref, k_refs, v_refs, o_ref,
                 m_s, l_s, acc_s, lam_init):
    G = len(k_refs)

    @pl.when(step == 0)
    def _():
        m_s[...] = jnp.full(m_s.shape, -jnp.inf, F32)
        l_s[...] = jnp.zeros_like(l_s)
        acc_s[...] = jnp.zeros_like(acc_s)

    qt = qt_ref[...]
    scores = []
    for i in range(G):
        s = lax.dot_general(qt, k_refs[i][...].astype(BF16), (((1,), (1,)), ((), ())),
                            preferred_element_type=F32)
        if i == G - 1:
            s = s + jnp.where(step == nsteps - 1, bias_ref[1], bias_ref[0])
        else:
            s = s + bias_ref[0]
        scores.append(s)
    m_old = m_s[...]
    m_new = m_old
    for s in scores:
        m_new = jnp.maximum(m_new, jnp.max(s, axis=-1, keepdims=True))
    alpha = jnp.exp(m_old - m_new)
    l_new = alpha * l_s[...]
    acc = alpha * acc_s[...]
    for i in range(G):
        p = jnp.exp(scores[i] - m_new)
        l_new = l_new + jnp.sum(p, axis=-1, keepdims=True)
        acc = acc + jnp.dot(p.astype(BF16), v_refs[i][...].astype(BF16), preferred_element_type=F32)
    m_s[...] = m_new
    l_s[...] = l_new
    acc_s[...] = acc

    @pl.when(step == nsteps - 1)
    def _():
        kn = knew_ref[...].astype(BF16).astype(F32)
        kn = jnp.concatenate([kn, kn], axis=0)
        vn = jnp.concatenate([vnew_ref[...], vnew_ref[...]], axis=0)
        s = jnp.sum(qt.astype(F32) * kn, axis=-1, keepdims=True) + bias_ref[2][:, 0:1]
        m_o = m_s[...]
        m_n = jnp.maximum(m_o, s)
        a = jnp.exp(m_o - m_n)
        p = jnp.exp(s - m_n)
        l_f = a * l_s[...] + p
        acc_f = (a * acc_s[...] + p * vn) / l_f
        o = acc_f[0:H_DIFF] - _lambda(lam_ref, lam_init) * acc_f[H_DIFF:2 * H_DIFF]
        o_ref[...] = _rms(o, gain_ref[...]) * (1.0 - lam_init)


def _decode_attn(qt, knew, vnew, dec_bias, lam_rows, gain_df, pages_k, pages_v, page_table, layer,
                 lam_init, *, G=16, name):
    DB, n_pages = page_table.shape
    PH = pages_k.shape[2]
    R = 2 * H_DIFF
    G = min(G, n_pages)
    assert n_pages % G == 0

    def pspec(i):
        return pl.BlockSpec((None, None, PH, LANES), lambda b, s, pt: (layer, pt[b, s * G + i], 0, 0))

    def per_b(rows):
        return pl.BlockSpec((None, rows, LANES), lambda b, s, pt: (b, 0, 0))

    row = pl.BlockSpec((1, LANES), lambda b, s, pt: (0, 0))
    blocks = [((PH, LANES), F32)] * (2 * G) + [((3, R, PH), F32)]
    grid_spec = pltpu.PrefetchScalarGridSpec(
        num_scalar_prefetch=1,
        grid=(DB, n_pages // G),
        in_specs=[per_b(R), per_b(H_DIFF), per_b(H_DIFF),
                  pl.BlockSpec((3, R, PH), lambda b, s, pt: (0, 0, 0)),
                  pl.BlockSpec((SUBLANES, LANES), lambda b, s, pt: (0, 0)), row]
        + [pspec(i) for i in range(G)] + [pspec(i) for i in range(G)],
        out_specs=per_b(H_DIFF),
        scratch_shapes=[pltpu.VMEM((R, 1), F32), pltpu.VMEM((R, 1), F32), pltpu.VMEM((R, LANES), F32)],
    )
    return pl.pallas_call(
        functools.partial(_decode_kernel, G=G, lam_init=lam_init),
        grid_spec=grid_spec,
        out_shape=jax.ShapeDtypeStruct((DB, H_DIFF, LANES), F32),
        compiler_params=pltpu.CompilerParams(
            dimension_semantics=("parallel", "arbitrary"), vmem_limit_bytes=_vmem_limit(blocks)),
        name=name,
    )(page_table, qt, knew, vnew, dec_bias, lam_rows, gain_df.reshape(1, LANES),
      *([pages_k] * G), *([pages_v] * G))


def _mlp_decode_kernel(pt_ref, x_ref, g_ref, wu_ref, wd_ref, *rest, G, n_groups, lam_init, has_final):
    del pt_ref
    rest = list(rest)
    gf_ref = rest.pop(0) if has_final else None
    qt_ref, knew_ref, vnew_ref, bias_ref, lam_ref, gain_ref = rest[:6]
    k_refs, v_refs = rest[6:6 + G], rest[6 + G:6 + 2 * G]
    o_ref, od_ref, h_ref, acc_ref, m_s, l_s, a_s = rest[6 + 2 * G:]
    _mlp_step(x_ref, g_ref, wu_ref, wd_ref, gf_ref, o_ref, h_ref, acc_ref)
    flat = pl.program_id(0) * pl.num_programs(1) + pl.program_id(1)
    _decode_step(flat % n_groups, n_groups, qt_ref, knew_ref, vnew_ref, bias_ref, lam_ref, gain_ref,
                 k_refs, v_refs, od_ref, m_s, l_s, a_s, lam_init)


def _fused_pages_per_step(M, FF, DB, n_pages, tm=512, tf=1024):
    steps = (M // min(tm, M)) * (FF // min(tf, FF))
    total = DB * n_pages
    if total % steps or n_pages % (total // steps):
        return None
    return total // steps


def _mlp_decode(x, gain, w_up, w_down, qt, knew, vnew, dec_bias, lam_rows, gain_df, pages_k, pages_v, page_table,
                layer, lam_init, *, final_gain=None, tm=512, tf=1024, name):
    M, D = x.shape
    FF = w_up.shape[1]
    tm, tf = min(tm, M), min(tf, FF)
    nf = FF // tf
    DB, n_pages = page_table.shape
    G = _fused_pages_per_step(M, FF, DB, n_pages, tm, tf)
    assert G is not None and M % tm == 0 and FF % tf == 0
    n_groups = n_pages // G
    PH = pages_k.shape[2]
    R = 2 * H_DIFF

    def pspec(k):
        return pl.BlockSpec((None, None, PH, LANES),
                            lambda i, f, pt: (layer, pt[(i * nf + f) // n_groups, ((i * nf + f) % n_groups) * G + k], 0, 0))

    def per_b(rows):
        return pl.BlockSpec((None, rows, LANES), lambda i, f, pt: ((i * nf + f) // n_groups, 0, 0))

    in_specs = [pl.BlockSpec((tm, D), lambda i, f, pt: (i, 0)),
                pl.BlockSpec((1, D), lambda i, f, pt: (0, 0)),
                pl.BlockSpec((D, tf), lambda i, f, pt: (0, f)),
                pl.BlockSpec((tf, D), lambda i, f, pt: (f, 0))]
    args = [x, gain.reshape(1, D), w_up, w_down]
    if final_gain is not None:
        in_specs.append(pl.BlockSpec((1, D), lambda i, f, pt: (0, 0)))
        args.append(final_gain.reshape(1, D))
    in_specs += [per_b(R), per_b(H_DIFF), per_b(H_DIFF),
                 pl.BlockSpec((3, R, PH), lambda i, f, pt: (0, 0, 0)),
                 pl.BlockSpec((SUBLANES, LANES), lambda i, f, pt: (0, 0)),
                 pl.BlockSpec((1, LANES), lambda i, f, pt: (0, 0))]
    in_specs += [pspec(k) for k in range(G)] + [pspec(k) for k in range(G)]
    args += [qt, knew, vnew, dec_bias, lam_rows, gain_df.reshape(1, LANES)] + [pages_k] * G + [pages_v] * G
    blocks = ([((tm, D), F32), ((D, tf), BF16), ((tf, D), BF16), ((tm, D), F32)]
              + [((PH, LANES), F32)] * (2 * G) + [((3, R, PH), F32)])
    resident = [((tm, D), BF16), ((tm, D), F32)]
    grid_spec = pltpu.PrefetchScalarGridSpec(
        num_scalar_prefetch=1,
        grid=(M // tm, nf),
        in_specs=in_specs,
        out_specs=[pl.BlockSpec((tm, D), lambda i, f, pt: (i, 0)), per_b(H_DIFF)],
        scratch_shapes=[pltpu.VMEM((tm, D), BF16), pltpu.VMEM((tm, D), F32),
                        pltpu.VMEM((R, 1), F32), pltpu.VMEM((R, 1), F32), pltpu.VMEM((R, LANES), F32)],
    )
    return pl.pallas_call(
        functools.partial(_mlp_decode_kernel, G=G, n_groups=n_groups, lam_init=lam_init,
                          has_final=final_gain is not None),
        grid_spec=grid_spec,
        out_shape=[jax.ShapeDtypeStruct((M, D), F32), jax.ShapeDtypeStruct((DB, H_DIFF, LANES), F32)],
        compiler_params=pltpu.CompilerParams(
            dimension_semantics=("arbitrary", "arbitrary"), vmem_limit_bytes=_vmem_limit(blocks, resident)),
        name=name,
    )(page_table, *args)


def _pad_rows(x, rows):
    return jnp.pad(x, ((0, rows - x.shape[0]), (0, 0)))


def kernel(x_prompt, x_sample, cache_k, cache_v, state_delta, state_conv, page_table, rel_bias, norm_mix, w_in, conv_w, a_log, dt_bias, norm_dn, w_o_dn, lam_q1, lam_k1, lam_q2, lam_k2, norm_diff, w_o_diff, w_out, norm_mlp, w_up, w_down, norm_final):
    B, T, D = x_prompt.shape
    DB = x_sample.shape[0]
    depth = w_in.shape[0]
    n_pool, page = cache_k.shape[1], cache_k.shape[2]
    conv_dim = conv_w.shape[-1]
    n_ba = 2 * H_DN
    assert conv_dim == 3 * H_DN * DN_DK and x_sample.shape[1] == 1
    c_gate = conv_dim + n_ba
    c_kdf = c_gate + H_DN * DN_DV + H_DIFF * 2 * DIFF_DQ
    c_gbr = c_kdf + H_DIFF * 2 * DIFF_DQ + H_DIFF * DIFF_DV
    f_widths = (H_DIFF * 2 * DIFF_DQ, H_DIFF * DIFF_DV, LANES)
    MS = 16
    tq = min(512, T)
    n_tail = B * (CONV_W - 1)
    assert n_tail <= MS

    xp = x_prompt.reshape(B * T, D)
    xs = _pad_rows(x_sample.reshape(DB, D), MS)
    diag, sub = _bias_tiles(rel_bias, tq, tq)
    fuse_mlp_decode = _fused_pages_per_step(B * T, w_up.shape[-1], DB, page_table.shape[1]) is not None
    dec_bias = _dec_bias(rel_bias, page)
    pages_k = cache_k.reshape(depth, n_pool, page * H_DIFF, 2 * DIFF_DQ)
    pages_v = cache_v.reshape(depth, n_pool, page * H_DIFF, DIFF_DV)
    in_map0 = jnp.arange(2 * DIFF_DQ) < DIFF_DQ

    kp_l, vp_l, sp_l, cp_l, ks_l, vs_l, ss_l, cs_l = [], [], [], [], [], [], [], []
    for l in range(depth):
        lam_init = 0.8 - 0.6 * math.exp(-0.3 * l)
        wl = w_in[l]
        w_b = jnp.concatenate([wl[:, :conv_dim], wl[:, c_gate:c_kdf], wl[:, c_gbr:]], axis=1).astype(BF16)
        w_f = jnp.pad(jnp.concatenate([wl[:, c_kdf:c_gbr], wl[:, conv_dim:c_gate]], axis=1),
                      ((0, 0), (0, LANES - n_ba))).astype(BF16)
        w_odn, w_odf, w_o = w_o_dn[l].astype(BF16), w_o_diff[l].astype(BF16), w_out[l].astype(BF16)
        w_u, w_d = w_up[l].astype(BF16), w_down[l].astype(BF16)
        alog_row = jnp.pad(a_log[l], (H_DN, LANES - n_ba)).reshape(1, LANES)
        dtb_row = jnp.pad(dt_bias[l], (H_DN, LANES - n_ba)).reshape(1, LANES)
        lam_rows = jnp.pad(jnp.stack([lam_q1[l], lam_k1[l], lam_q2[l], lam_k2[l]]),
                           ((0, SUBLANES - 4), (0, LANES - DIFF_DQ)))
        last = l == depth - 1

        x_tail = _pad_rows(xp.reshape(B, T, D)[:, T - (CONV_W - 1):].reshape(n_tail, D), MS)
        zst = _mm(jnp.concatenate([xs, x_tail], axis=0), w_b, gain=norm_mix[l], tn=1024, name=f"in_proj_s{l}")
        zs, z_tail = zst[:MS], zst[MS:]
        zb = _mm(xp, w_b, gain=norm_mix[l], out_dtype=BF16, tn=1536, name=f"in_proj_p{l}")
        kf, vf, ba = _mm_split(xp, w_f, norm_mix[l], f_widths, name=f"in_proj_kv_p{l}")
        o_dn, s_fin = _delta_prompt(zb, ba, conv_w[l], alog_row, dtb_row, norm_dn[l], B, T, name=f"delta_p{l}")
        o_df = _attn_prompt(zb, kf, vf, diag, sub, lam_rows, norm_diff[l], B, T, lam_init, tq=tq,
                            name=f"attn_p{l}")
        merged = _merge(o_dn, o_df, zb, w_odn, w_odf, name=f"merge_p{l}")
        xp = _mm(merged, w_o, residual=xp, tn=1024, name=f"out_proj_p{l}")
        kp_l.append(kf.reshape(B, T, H_DIFF, 2 * DIFF_DQ))
        vp_l.append(vf.reshape(B, T, H_DIFF, DIFF_DV))
        sp_l.append(s_fin)
        cp_l.append(z_tail[:n_tail, :conv_dim].reshape(B, CONV_W - 1, conv_dim))

        ks16, vs16, bas16 = _mm_split(xs, w_f, norm_mix[l], f_widths, name=f"in_proj_kv_s{l}")
        zs3 = zs[:DB].reshape(DB, 1, -1)
        o_dn_s, s_new = _delta_step(zs3, bas16[:DB].reshape(DB, 1, LANES), state_conv,
                                    conv_w[l], alog_row, dtb_row, norm_dn[l], state_delta, l, name=f"delta_s{l}")
        q_s = zs[:DB, _C_QDF:_C_QDF + 1024].reshape(DB, H_DIFF, 2 * DIFF_DQ) * (DIFF_DQ ** -0.5)
        qt = jnp.concatenate([jnp.where(in_map0, q_s, 0.0), jnp.where(in_map0, 0.0, q_s)], axis=1).astype(BF16)
        k_s = ks16[:DB].reshape(DB, 1, -1)
        v_s = vs16[:DB].reshape(DB, 1, -1)
        dec_args = (qt, k_s.reshape(DB, H_DIFF, 2 * DIFF_DQ), v_s.reshape(DB, H_DIFF, DIFF_DV),
                    dec_bias, lam_rows, norm_diff[l], pages_k, pages_v, page_table, l, lam_init)
        final_gain = norm_final if last else None
        if fuse_mlp_decode:
            xp, o_df_s = _mlp_decode(xp, norm_mlp[l], w_u, w_d, *dec_args, final_gain=final_gain,
                                     name=f"mlp_p_attn_s{l}")
        else:
            xp = _mlp(xp, norm_mlp[l], w_u, w_d, final_gain=final_gain, name=f"mlp_p{l}")
            o_df_s = _decode_attn(*dec_args, name=f"attn_s{l}")
        merged_s = _merge(_pad_rows(o_dn_s.reshape(DB, -1), MS).astype(BF16),
                          _pad_rows(o_df_s.reshape(DB, -1), MS).astype(BF16), zs,
                          w_odn, w_odf, tn=1024, name=f"merge_s{l}")
        xs = _mm(merged_s, w_o, residual=xs, tn=1024, name=f"out_proj_s{l}")
        xs = _mlp(xs, norm_mlp[l], w_u, w_d, final_gain=norm_final if last else None, name=f"mlp_s{l}")
        ks_l.append(k_s.reshape(DB, 1, H_DIFF, 2 * DIFF_DQ))
        vs_l.append(v_s.reshape(DB, 1, H_DIFF, DIFF_DV))
        ss_l.append(s_new)
        cs_l.append(jnp.concatenate([state_conv[l][:, 1:], zs3[:, :, :conv_dim]], axis=1))

    y_prompt = xp.reshape(B, T, D)
    y_sample = xs[:DB].reshape(DB, 1, D)
    return (y_prompt, y_sample,
            jnp.stack(kp_l), jnp.stack(vp_l), jnp.stack(sp_l), jnp.stack(cp_l),
            jnp.stack(ks_l), jnp.stack(vs_l), jnp.stack(ss_l), jnp.stack(cs_l))
```

```python
import functools
import math

import jax
import jax.numpy as jnp
from jax import lax
from jax.experimental import pallas as pl
from jax.experimental.pallas import tpu as pltpu

F32 = jnp.float32
BF16 = jnp.bfloat16

H_DN = 8
DN_DK = 128
DN_DV = 128
CONV_W = 4
H_DIFF = 8
DIFF_DQ = 64
DIFF_DV = 128
NUM_BUCKETS = 32
MAX_DISTANCE = 128
EPS = 1e-6

LANES = 128
SUBLANES = 8
VMEM_BYTES_V7X = 64 * 1024 * 1024
VMEM_INTERNAL_ALLOWANCE = 12 * 1024 * 1024

_MAX_EXACT = NUM_BUCKETS // 2
_BUCKET_START = tuple(
    k if k <= _MAX_EXACT else math.ceil(
        _MAX_EXACT * (MAX_DISTANCE / _MAX_EXACT) ** ((k - _MAX_EXACT) / (NUM_BUCKETS - _MAX_EXACT)))
    for k in range(NUM_BUCKETS))
_FAR_REL = _BUCKET_START[-1]

_C_QDN, _C_KDN, _C_VDN, _C_GATE = 0, 1024, 2048, 3072
_C_QDF = 4096
_C_GA = 5120


def _nbytes(shape, dtype):
    return math.prod(shape) * jnp.dtype(dtype).itemsize


def _vmem_limit(pipelined, resident=()):
    need = 2 * sum(_nbytes(s, d) for s, d in pipelined) + sum(_nbytes(s, d) for s, d in resident)
    return int(min(need + VMEM_INTERNAL_ALLOWANCE, VMEM_BYTES_V7X - 2 * 1024 * 1024))


def _dot(a, b):
    return jnp.dot(a.astype(BF16), b.astype(BF16), preferred_element_type=F32)


def _dot_nt(a, b):
    return lax.dot_general(a.astype(BF16), b.astype(BF16), (((1,), (1,)), ((), ())),
                           preferred_element_type=F32)


def _dot_tn(a, b):
    return lax.dot_general(a.astype(BF16), b.astype(BF16), (((0,), (0,)), ((), ())),
                           preferred_element_type=F32)


def _bdot(a, b):
    return lax.dot_general(a.astype(BF16), b.astype(BF16), (((2,), (1,)), ((0,), (0,))),
                           preferred_element_type=F32)


def _bdot_nt(a, b):
    return lax.dot_general(a.astype(BF16), b.astype(BF16), (((2,), (2,)), ((0,), (0,))),
                           preferred_element_type=F32)


def _bdot_tn(a, b):
    return lax.dot_general(a.astype(BF16), b.astype(BF16), (((1,), (1,)), ((0,), (0,))),
                           preferred_element_type=F32)


def _rms(x, gain):
    return x * lax.rsqrt(jnp.mean(x * x, axis=-1, keepdims=True) + EPS) * gain


def _silu(x):
    h = 0.5 * x
    return h + h * jnp.tanh(h)


def _softplus(x):
    return jnp.maximum(x, 0.0) + jnp.log1p(jnp.exp(-jnp.abs(x)))


def _norm_rows_to(x_ref, g_ref, h_ref):
    tm = x_ref.shape[0]
    rows = min(tm, 128)

    def body(i, c):
        sl = pl.ds(pl.multiple_of(i * rows, rows), rows)
        h_ref[sl, :] = _rms(x_ref[sl, :], g_ref[...]).astype(h_ref.dtype)
        return c

    lax.fori_loop(0, tm // rows, body, 0)


def _mm_kernel(*refs, has_gain, has_res):
    it = iter(refs)
    x_ref = next(it)
    g_ref = next(it) if has_gain else None
    w_ref = next(it)
    r_ref = next(it) if has_res else None
    o_ref = next(it)
    if has_gain:
        h_ref = next(it)

        @pl.when(pl.program_id(1) == 0)
        def _():
            _norm_rows_to(x_ref, g_ref, h_ref)

        lhs = h_ref[...]
    else:
        lhs = x_ref[...]
    acc = jnp.dot(lhs, w_ref[...], preferred_element_type=F32)
    if has_res:
        acc = acc + r_ref[...]
    o_ref[...] = acc.astype(o_ref.dtype)


def _mm(x, w, *, gain=None, residual=None, out_dtype=F32, tm=1024, tn=512, name):
    M, K = x.shape
    N = w.shape[1]
    tm, tn = min(tm, M), min(tn, N)
    assert M % tm == 0 and N % tn == 0, (M, N, tm, tn)
    in_specs = [pl.BlockSpec((tm, K), lambda i, j: (i, 0))]
    args = [x]
    blocks = [((tm, K), x.dtype), ((K, tn), w.dtype), ((tm, tn), out_dtype)]
    scratch = []
    if gain is not None:
        in_specs.append(pl.BlockSpec((1, K), lambda i, j: (0, 0)))
        args.append(gain.reshape(1, K))
        scratch.append(pltpu.VMEM((tm, K), BF16))
    in_specs.append(pl.BlockSpec((K, tn), lambda i, j: (0, j)))
    args.append(w)
    if residual is not None:
        in_specs.append(pl.BlockSpec((tm, tn), lambda i, j: (i, j)))
        args.append(residual)
        blocks.append(((tm, tn), residual.dtype))
    return pl.pallas_call(
        functools.partial(_mm_kernel, has_gain=gain is not None, has_res=residual is not None),
        grid=(M // tm, N // tn),
        in_specs=in_specs,
        out_specs=pl.BlockSpec((tm, tn), lambda i, j: (i, j)),
        out_shape=jax.ShapeDtypeStruct((M, N), out_dtype),
        scratch_shapes=scratch,
        compiler_params=pltpu.CompilerParams(
            dimension_semantics=("parallel", "arbitrary"),
            vmem_limit_bytes=_vmem_limit(blocks, [((tm, K), BF16)] if gain is not None else [])),
        name=name,
    )(*args)


def _mm_split_kernel(x_ref, g_ref, w_ref, *rest, widths):
    out_refs, h_ref = rest[:-1], rest[-1]
    _norm_rows_to(x_ref, g_ref, h_ref)
    acc = jnp.dot(h_ref[...], w_ref[...], preferred_element_type=F32)
    col = 0
    for o_ref, width in zip(out_refs, widths):
        o_ref[...] = acc[:, col:col + width]
        col += width


def _mm_split(x, w, gain, widths, *, tm=512, name):
    M, K = x.shape
    N = w.shape[1]
    tm = min(tm, M)
    assert M % tm == 0 and sum(widths) == N and all(wd % LANES == 0 for wd in widths)
    blocks = [((tm, K), x.dtype), ((K, N), w.dtype), ((tm, N), F32)]
    return pl.pallas_call(
        functools.partial(_mm_split_kernel, widths=tuple(widths)),
        grid=(M // tm,),
        in_specs=[pl.BlockSpec((tm, K), lambda i: (i, 0)), pl.BlockSpec((1, K), lambda i: (0, 0)),
                  pl.BlockSpec((K, N), lambda i: (0, 0))],
        out_specs=[pl.BlockSpec((tm, wd), lambda i: (i, 0)) for wd in widths],
        out_shape=[jax.ShapeDtypeStruct((M, wd), F32) for wd in widths],
        scratch_shapes=[pltpu.VMEM((tm, K), BF16)],
        compiler_params=pltpu.CompilerParams(
            dimension_semantics=("parallel",),
            vmem_limit_bytes=_vmem_limit(blocks, [((tm, K), BF16), ((tm, N), F32)])),
        name=name,
    )(x, gain.reshape(1, K), w)


def _merge_kernel(odn_ref, odf_ref, ga_ref, gb_ref, wdn_ref, wdf_ref, o_ref):
    a = jnp.dot(odn_ref[...], wdn_ref[...], preferred_element_type=F32)
    b = jnp.dot(odf_ref[...], wdf_ref[...], preferred_element_type=F32)
    ga = jax.nn.sigmoid(ga_ref[...].astype(F32))
    gb = jax.nn.sigmoid(gb_ref[...].astype(F32))
    o_ref[...] = (ga * a + gb * b).astype(o_ref.dtype)


def _merge(o_dn, o_df, z, w_dn, w_df, *, tm=1024, tn=1024, name):
    M, K = o_dn.shape
    N = w_dn.shape[1]
    tm, tn = min(tm, M), min(tn, N)
    assert M % tm == 0 and N % tn == 0 and _C_GA % tn == 0
    ga0 = _C_GA // tn
    gb0 = (_C_GA + N) // tn
    blocks = [((tm, K), BF16)] * 2 + [((tm, tn), z.dtype)] * 2 + [((K, tn), BF16)] * 2 + [((tm, tn), BF16)]
    return pl.pallas_call(
        _merge_kernel,
        grid=(M // tm, N // tn),
        in_specs=[
            pl.BlockSpec((tm, K), lambda i, j: (i, 0)),
            pl.BlockSpec((tm, K), lambda i, j: (i, 0)),
            pl.BlockSpec((tm, tn), lambda i, j: (i, ga0 + j)),
            pl.BlockSpec((tm, tn), lambda i, j: (i, gb0 + j)),
            pl.BlockSpec((K, tn), lambda i, j: (0, j)),
            pl.BlockSpec((K, tn), lambda i, j: (0, j)),
        ],
        out_specs=pl.BlockSpec((tm, tn), lambda i, j: (i, j)),
        out_shape=jax.ShapeDtypeStruct((M, N), BF16),
        compiler_params=pltpu.CompilerParams(
            dimension_semantics=("parallel", "arbitrary"), vmem_limit_bytes=_vmem_limit(blocks)),
        name=name,
    )(o_dn, o_df, z, z, w_dn, w_df)


def _mlp_kernel(*refs, has_final):
    if has_final:
        x_ref, g_ref, wu_ref, wd_ref, gf_ref, o_ref, h_ref, acc_ref = refs
    else:
        x_ref, g_ref, wu_ref, wd_ref, o_ref, h_ref, acc_ref = refs
        gf_ref = None
    _mlp_step(x_ref, g_ref, wu_ref, wd_ref, gf_ref, o_ref, h_ref, acc_ref)


def _mlp_step(x_ref, g_ref, wu_ref, wd_ref, gf_ref, o_ref, h_ref, acc_ref):
    has_final = gf_ref is not None
    f = pl.program_id(1)

    @pl.when(f == 0)
    def _():
        _norm_rows_to(x_ref, g_ref, h_ref)
        acc_ref[...] = jnp.zeros_like(acc_ref)

    u = jnp.dot(h_ref[...], wu_ref[...], preferred_element_type=F32)
    u = jnp.square(jnp.maximum(u, 0.0)).astype(BF16)
    acc_ref[...] += jnp.dot(u, wd_ref[...], preferred_element_type=F32)

    @pl.when(f == pl.num_programs(1) - 1)
    def _():
        y = x_ref[...] + acc_ref[...]
        if has_final:
            y = _rms(y, gf_ref[...])
        o_ref[...] = y


def _mlp(x, gain, w_up, w_down, *, final_gain=None, tm=512, tf=1024, name):
    M, D = x.shape
    FF = w_up.shape[1]
    tm, tf = min(tm, M), min(tf, FF)
    assert M % tm == 0 and FF % tf == 0
    in_specs = [
        pl.BlockSpec((tm, D), lambda i, f: (i, 0)),
        pl.BlockSpec((1, D), lambda i, f: (0, 0)),
        pl.BlockSpec((D, tf), lambda i, f: (0, f)),
        pl.BlockSpec((tf, D), lambda i, f: (f, 0)),
    ]
    args = [x, gain.reshape(1, D), w_up, w_down]
    if final_gain is not None:
        in_specs.append(pl.BlockSpec((1, D), lambda i, f: (0, 0)))
        args.append(final_gain.reshape(1, D))
    blocks = [((tm, D), F32), ((D, tf), BF16), ((tf, D), BF16), ((tm, D), F32)]
    resident = [((tm, D), BF16), ((tm, D), F32)]
    return pl.pallas_call(
        functools.partial(_mlp_kernel, has_final=final_gain is not None),
        grid=(M // tm, FF // tf),
        in_specs=in_specs,
        out_specs=pl.BlockSpec((tm, D), lambda i, f: (i, 0)),
        out_shape=jax.ShapeDtypeStruct((M, D), F32),
        scratch_shapes=[pltpu.VMEM((tm, D), BF16), pltpu.VMEM((tm, D), F32)],
        compiler_params=pltpu.CompilerParams(
            dimension_semantics=("parallel", "arbitrary"), vmem_limit_bytes=_vmem_limit(blocks, resident)),
        name=name,
    )(*args)


def _bias_of_rel(rb_ref, h, rel):
    far = rb_ref[NUM_BUCKETS - 1, h]
    v = jnp.full(rel.shape, rb_ref[0, h] - far, F32)
    for k in range(1, NUM_BUCKETS):
        v = jnp.where(rel >= _BUCKET_START[k], rb_ref[k, h] - far, v)
    return v


def _bias_tiles_kernel(rb_ref, diag_ref, sub_ref, *, tk):
    h = pl.program_id(0)
    shape = diag_ref.shape
    rel = lax.broadcasted_iota(jnp.int32, shape, 0) - lax.broadcasted_iota(jnp.int32, shape, 1)
    diag_ref[...] = jnp.where(rel >= 0, _bias_of_rel(rb_ref, h, rel), -jnp.inf)
    sub_ref[...] = _bias_of_rel(rb_ref, h, rel + tk)


def _bias_tiles(rel_bias, tq, tk):
    assert tq == tk and tk + 1 >= _FAR_REL
    H = rel_bias.shape[1]
    return pl.pallas_call(
        functools.partial(_bias_tiles_kernel, tk=tk),
        grid=(H,),
        in_specs=[pl.BlockSpec(memory_space=pltpu.SMEM)],
        out_specs=[pl.BlockSpec((None, tq, tk), lambda h: (h, 0, 0))] * 2,
        out_shape=[jax.ShapeDtypeStruct((H, tq, tk), F32)] * 2,
        name="t5_bias_tiles",
    )(rel_bias)


def _dec_bias_kernel(rb_ref, o_ref, *, page):
    shape = o_ref.shape[1:]
    row_head = lax.broadcasted_iota(jnp.int32, shape, 0) & (H_DIFF - 1)
    lane = lax.broadcasted_iota(jnp.int32, shape, 1)
    own_head = (lane & (H_DIFF - 1)) == row_head
    rel = page - (lane >> (H_DIFF.bit_length() - 1))
    last = jnp.zeros(shape, F32)
    new = jnp.zeros(shape, F32)
    for h in range(H_DIFF):
        sel = row_head == h
        last = jnp.where(sel, _bias_of_rel(rb_ref, h, rel), last)
        new = jnp.where(sel, rb_ref[0, h] - rb_ref[NUM_BUCKETS - 1, h], new)
    o_ref[0] = jnp.where(own_head, 0.0, -jnp.inf)
    o_ref[1] = jnp.where(own_head, last, -jnp.inf)
    o_ref[2] = new


def _dec_bias(rel_bias, page):
    assert page + 1 >= _FAR_REL and H_DIFF & (H_DIFF - 1) == 0
    return pl.pallas_call(
        functools.partial(_dec_bias_kernel, page=page),
        in_specs=[pl.BlockSpec(memory_space=pltpu.SMEM)],
        out_specs=pl.BlockSpec(memory_space=pltpu.VMEM),
        out_shape=jax.ShapeDtypeStruct((3, 2 * H_DIFF, page * H_DIFF), F32),
        name="t5_bias_decode",
    )(rel_bias)


def _unit_lower_inverse(L, ri, ci):
    C = L.shape[-1]
    eye = (ri == ci).astype(F32)
    T = None
    lb = 0
    while (1 << lb) < C:
        below = ((ri >> (lb + 1)) == (ci >> (lb + 1))) & (((ri >> lb) & 1) == 1) & (((ci >> lb) & 1) == 0)
        Lb = jnp.where(below, L, 0.0)
        T = eye - Lb if T is None else T - _bdot(_bdot(T, Lb), T)
        lb += 1
    return T


def _delta_chunk_local(q, k, v, beta, g):
    N, C, _ = q.shape
    ri = lax.broadcasted_iota(jnp.int32, (N, C, C), 1)
    ci = lax.broadcasted_iota(jnp.int32, (N, C, C), 2)
    tril = ri >= ci
    g_cols = jnp.broadcast_to(g, (N, C, C))
    g_row = jnp.sum(jnp.where(ri == ci, g_cols, 0.0), axis=1, keepdims=True)
    cum_col = jnp.sum(jnp.where(tril, jnp.broadcast_to(g_row, (N, C, C)), 0.0), axis=2, keepdims=True)
    cum_row = jnp.sum(jnp.where(ri <= ci, g_cols, 0.0), axis=1, keepdims=True)
    decay = jnp.where(tril, jnp.exp(jnp.where(tril, cum_col - cum_row, 0.0)), 0.0)
    g_last = jnp.sum(g, axis=1, keepdims=True)
    e_col = jnp.exp(cum_col)
    kb = k * beta
    vb = v * beta
    L = jnp.where(ri > ci, _bdot_nt(kb, k) * decay, 0.0)
    T = _unit_lower_inverse(L, ri, ci)
    vk = _bdot(T, jnp.concatenate([vb, kb * e_col], axis=2))
    qk = _bdot_nt(q, k) * decay
    return (vk[:, :, :DN_DV], vk[:, :, DN_DV:], q * e_col, qk, k * jnp.exp(g_last - cum_col),
            jnp.broadcast_to(jnp.exp(g_last), (N, SUBLANES, LANES)))


def _delta_kernel(zq_ref, zk_ref, zv_ref, zg_ref, ba_ref, wq_ref, wk_ref, wv_ref, alog_ref, dtb_ref,
                  gain_ref, *rest, HG, TT, C, cast_weights):
    if cast_weights:
        wu32_ref, wd32_ref, o_ref, sout_ref, wu16_ref, wd16_ref = rest[:6]
        rest = rest[6:]
        wu16_ref[...] = wu32_ref[...].astype(BF16)
        wd16_ref[...] = wd32_ref[...].astype(BF16)
    else:
        o_ref, sout_ref = rest[:2]
        rest = rest[2:]
    xq_s, xk_s, xv_s, q_s, k_s, v_s, S_s, val_s, kcum_s, qe_s, qk_s, kdec_s, egl_s = rest
    hgi = pl.program_id(1)
    t = pl.program_id(2)
    W = HG * LANES
    NC = TT // C

    @pl.when(t == 0)
    def _():
        for x_s in (xq_s, xk_s, xv_s):
            x_s[0:SUBLANES, :] = jnp.zeros((SUBLANES, W), F32)
        S_s[...] = jnp.zeros_like(S_s)

    for z_ref, x_s, w_ref, dst in ((zq_ref, xq_s, wq_ref, q_s), (zk_ref, xk_s, wk_ref, k_s),
                                   (zv_ref, xv_s, wv_ref, v_s)):
        x_s[SUBLANES:SUBLANES + TT, :] = z_ref[...].astype(F32)
        xfull = x_s[...]
        acc = xfull[SUBLANES:, :] * w_ref[CONV_W - 1:CONV_W, :]
        for j in range(CONV_W - 1):
            shifted = pltpu.roll(xfull, CONV_W - 1 - j, axis=0)[SUBLANES:, :]
            acc = acc + shifted * w_ref[j:j + 1, :]
        dst[...] = _silu(acc)
        x_s[0:SUBLANES, :] = x_s[TT:TT + SUBLANES, :]

    ba = ba_ref[...]
    bt = jax.nn.sigmoid(ba)
    gt = -jnp.exp(alog_ref[...]) * _softplus(ba + dtb_ref[...])
    lane = lax.broadcasted_iota(jnp.int32, (TT, LANES), 1)
    qs, ks, vs, betas, gs = [], [], [], [], []
    for hh in range(HG):
        lanes = slice(hh * LANES, (hh + 1) * LANES)
        head = hgi * HG + hh
        qh = q_s[:, lanes]
        qh = qh * lax.rsqrt(jnp.sum(qh * qh, axis=-1, keepdims=True) + EPS) * (DN_DK ** -0.5)
        kh = k_s[:, lanes]
        kh = kh * lax.rsqrt(jnp.sum(kh * kh, axis=-1, keepdims=True) + EPS)
        qs.append(qh.reshape(NC, C, LANES))
        ks.append(kh.reshape(NC, C, LANES))
        vs.append(v_s[:, lanes].reshape(NC, C, LANES))
        betas.append(jnp.sum(jnp.where(lane == head, bt, 0.0), axis=-1, keepdims=True).reshape(NC, C, 1))
        gs.append(jnp.sum(jnp.where(lane == head + H_DN, gt, 0.0), axis=-1, keepdims=True).reshape(NC, C, 1))

    cat = functools.partial(jnp.concatenate, axis=0)
    local = _delta_chunk_local(cat(qs), cat(ks), cat(vs), cat(betas), cat(gs))
    for ref, val in zip((val_s, kcum_s, qe_s, qk_s, kdec_s, egl_s), local):
        ref[...] = val.reshape(ref.shape)

    for c in range(NC):
        rows = slice(c * C, (c + 1) * C)
        S = S_s[...]
        r = _bdot(jnp.concatenate([kcum_s[:, c], qe_s[:, c]], axis=1), S)
        v_new = val_s[:, c] - r[:, :C]
        o = r[:, C:] + _bdot(qk_s[:, c], v_new)
        S_s[...] = S * egl_s[:, c, 0:1, 0:1] + _bdot_tn(kdec_s[:, c], v_new)
        for hh in range(HG):
            lanes = slice(hh * LANES, (hh + 1) * LANES)
            gate = _silu(zg_ref[rows, lanes].astype(F32))
            o_ref[rows, lanes] = (_rms(o[hh], gain_ref[...]) * gate).astype(o_ref.dtype)

    @pl.when(t == pl.num_programs(2) - 1)
    def _():
        sout_ref[...] = S_s[...]


def _mlp_weight_slab(FF, steps):
    if FF % steps or (FF // steps) % LANES:
        return None
    return FF // steps


def _delta_prompt(zb, ba, conv_w, alog_row, dtb_row, gain_dn, B, T, mlp_weights, *, HG=8, TT=256, C=64, name):
    M = zb.shape[0]
    TT = min(TT, T)
    C = min(C, TT)
    assert T % TT == 0 and TT % C == 0 and H_DN % HG == 0 and C & (C - 1) == 0 and C % 16 == 0
    NT = T // TT
    NC = TT // C
    W = HG * LANES
    nhg = H_DN // HG

    def zspec(col0):
        return pl.BlockSpec((TT, W), lambda b, hg, t: (b * NT + t, col0 // W + hg))

    def wspec(col0):
        return pl.BlockSpec((CONV_W, W), lambda b, hg, t: (0, col0 // W + hg))

    row = pl.BlockSpec((1, LANES), lambda b, hg, t: (0, 0))
    blocks = [((TT, W), zb.dtype)] * 4 + [((TT, LANES), F32)] + [((TT, W), BF16), ((HG, DN_DK, DN_DV), F32)]
    scratch = ([((TT + SUBLANES, W), F32)] * 3 + [((TT, W), F32)] * 3 + [((HG, DN_DK, DN_DV), F32)]
               + [((HG, NC, C, LANES), F32)] * 3 + [((HG, NC, C, C), F32)] + [((HG, NC, C, LANES), F32)]
               + [((HG, NC, SUBLANES, LANES), F32)])
    temporaries = [((HG * NC, C, LANES), F32)] * 12
    in_specs = [zspec(_C_QDN), zspec(_C_KDN), zspec(_C_VDN), zspec(_C_GATE),
                pl.BlockSpec((TT, LANES), lambda b, hg, t: (b * NT + t, 0)),
                wspec(0), wspec(H_DN * DN_DK), wspec(2 * H_DN * DN_DK), row, row, row]
    out_specs = [pl.BlockSpec((TT, W), lambda b, hg, t: (b * NT + t, hg)),
                 pl.BlockSpec((None, HG, DN_DK, DN_DV), lambda b, hg, t: (b, hg, 0, 0))]
    out_shape = [jax.ShapeDtypeStruct((M, H_DN * DN_DV), BF16),
                 jax.ShapeDtypeStruct((B, H_DN, DN_DK, DN_DV), F32)]
    args = [zb, zb, zb, zb, ba, conv_w, conv_w, conv_w, alog_row, dtb_row, gain_dn.reshape(1, LANES)]
    w_up_all, w_down_all, layer = mlp_weights
    D, FF = w_up_all.shape[1:]
    slab = _mlp_weight_slab(FF, B * nhg * NT)
    if slab is not None:
        def step(b, hg, t):
            return (b * nhg + hg) * NT + t

        in_specs += [pl.BlockSpec((None, D, slab), lambda b, hg, t: (layer, 0, step(b, hg, t))),
                     pl.BlockSpec((None, slab, D), lambda b, hg, t: (layer, step(b, hg, t), 0))]
        out_specs += [pl.BlockSpec((D, slab), lambda b, hg, t: (0, step(b, hg, t))),
                      pl.BlockSpec((slab, D), lambda b, hg, t: (step(b, hg, t), 0))]
        out_shape += [jax.ShapeDtypeStruct((D, FF), BF16), jax.ShapeDtypeStruct((FF, D), BF16)]
        args += [w_up_all, w_down_all]
        blocks = blocks + [((D, slab), F32), ((slab, D), F32), ((D, slab), BF16), ((slab, D), BF16)]
    outs = pl.pallas_call(
        functools.partial(_delta_kernel, HG=HG, TT=TT, C=C, cast_weights=slab is not None),
        grid=(B, nhg, NT),
        in_specs=in_specs,
        out_specs=out_specs,
        out_shape=out_shape,
        scratch_shapes=[pltpu.VMEM(shape, dtype) for shape, dtype in scratch],
        compiler_params=pltpu.CompilerParams(
            dimension_semantics=("parallel", "parallel", "arbitrary"),
            vmem_limit_bytes=_vmem_limit(blocks, scratch + temporaries)),
        name=name,
    )(*args)
    if slab is None:
        outs = [*outs, w_up_all[layer].astype(BF16), w_down_all[layer].astype(BF16)]
    return outs


def _lambda(lam_ref, lam_init):
    lv = lam_ref[...]
    s1 = jnp.sum(lv[0:1, :] * lv[1:2, :], axis=-1, keepdims=True)
    s2 = jnp.sum(lv[2:3, :] * lv[3:4, :], axis=-1, keepdims=True)
    return jnp.exp(s1) - jnp.exp(s2) + lam_init


def _attn_kernel(zq_ref, zk_ref, zv_ref, diag_ref, sub_ref, lam_ref, gain_ref, o_ref, kb_s, vb_s, *, tq, lam_init):
    T = zk_ref.shape[0]
    kb_s[...] = zk_ref[...].astype(BF16)
    vb_s[...] = zv_ref[...].astype(BF16)
    lam = _lambda(lam_ref, lam_init)
    lane = lax.broadcasted_iota(jnp.int32, (tq, LANES), 1)

    def scores(qi, mi):
        q = zq_ref[qi * tq:(qi + 1) * tq, :].astype(F32) * (DIFF_DQ ** -0.5)
        in_map = (lane >= DIFF_DQ) if mi else (lane < DIFF_DQ)
        qm = jnp.where(in_map, q, 0.0).astype(BF16)
        s = lax.dot_general(qm, kb_s[0:(qi + 1) * tq, :], (((1,), (1,)), ((), ())), preferred_element_type=F32)
        parts = []
        if qi >= 2:
            parts.append(s[:, :(qi - 1) * tq])
        if qi >= 1:
            parts.append(s[:, (qi - 1) * tq:qi * tq] + sub_ref[...])
        parts.append(s[:, qi * tq:] + diag_ref[...])
        return parts[0] if len(parts) == 1 else jnp.concatenate(parts, axis=1)

    items = [(qi, mi) for qi in range(T // tq) for mi in range(2)]
    s_next = scores(*items[0])
    outs = []
    for idx, (qi, mi) in enumerate(items):
        s = s_next
        if idx + 1 < len(items):
            s_next = scores(*items[idx + 1])
        p = jnp.exp(s - jnp.max(s, axis=-1, keepdims=True))
        l = jnp.sum(p, axis=-1, keepdims=True)
        outs.append(jnp.dot(p.astype(BF16), vb_s[0:(qi + 1) * tq, :], preferred_element_type=F32) / l)
        if mi == 1:
            o = outs[0] - lam * outs[1]
            outs = []
            o_ref[qi * tq:(qi + 1) * tq, :] = (_rms(o, gain_ref[...]) * (1.0 - lam_init)).astype(o_ref.dtype)


def _attn_prompt(zb, kf, vf, diag, sub, lam_rows, gain_df, B, T, lam_init, *, tq, name):
    M = zb.shape[0]
    assert T % tq == 0
    row = pl.BlockSpec((1, LANES), lambda b, h: (0, 0))
    blocks = [((T, LANES), zb.dtype)] + [((T, LANES), F32)] * 2 + [((tq, tq), F32)] * 2 + [((T, LANES), BF16)]
    resident = [((T, LANES), BF16)] * 2 + [((tq, T), F32)] * 4
    return pl.pallas_call(
        functools.partial(_attn_kernel, tq=tq, lam_init=lam_init),
        grid=(B, H_DIFF),
        in_specs=[pl.BlockSpec((T, LANES), lambda b, h: (b, _C_QDF // LANES + h)),
                  pl.BlockSpec((T, LANES), lambda b, h: (b, h)),
                  pl.BlockSpec((T, LANES), lambda b, h: (b, h)),
                  pl.BlockSpec((None, tq, tq), lambda b, h: (h, 0, 0)),
                  pl.BlockSpec((None, tq, tq), lambda b, h: (h, 0, 0)),
                  pl.BlockSpec((SUBLANES, LANES), lambda b, h: (0, 0)), row],
        out_specs=pl.BlockSpec((T, LANES), lambda b, h: (b, h)),
        out_shape=jax.ShapeDtypeStruct((M, H_DIFF * DIFF_DV), BF16),
        scratch_shapes=[pltpu.VMEM((T, LANES), BF16)] * 2,
        compiler_params=pltpu.CompilerParams(
            dimension_semantics=("parallel", "parallel"),
            vmem_limit_bytes=_vmem_limit(blocks, resident)),
        name=name,
    )(zb, kf, vf, diag, sub, lam_rows, gain_df.reshape(1, LANES))


def _delta_step_kernel(zq_ref, zk_ref, zv_ref, zg_ref, ba_ref, cq_ref, ck_ref, cv_ref, wq_ref, wk_ref, wv_ref,
                       alog_ref, dtb_ref, gain_ref, s_ref, o_ref, sout_ref):
    def conv(z_ref, c_ref, w_ref):
        acc = z_ref[...] * w_ref[CONV_W - 1:CONV_W, :]
        for j in range(CONV_W - 1):
            acc = acc + c_ref[j:j + 1, :] * w_ref[j:j + 1, :]
        return _silu(acc)

    cq, ck, cv = conv(zq_ref, cq_ref, wq_ref), conv(zk_ref, ck_ref, wk_ref), conv(zv_ref, cv_ref, wv_ref)
    ba = ba_ref[...]
    bt = jax.nn.sigmoid(ba)
    gt = -jnp.exp(alog_ref[...]) * _softplus(ba + dtb_ref[...])
    lane = lax.broadcasted_iota(jnp.int32, (1, LANES), 1)
    pad = jnp.zeros((SUBLANES - 2, LANES), F32)
    for h in range(H_DN):
        lanes = slice(h * LANES, (h + 1) * LANES)
        q = cq[:, lanes]
        q = q * lax.rsqrt(jnp.sum(q * q, axis=-1, keepdims=True) + EPS) * (DN_DK ** -0.5)
        k = ck[:, lanes]
        k = k * lax.rsqrt(jnp.sum(k * k, axis=-1, keepdims=True) + EPS)
        v = cv[:, lanes]
        beta = jnp.sum(jnp.where(lane == h, bt, 0.0), axis=-1, keepdims=True)
        eg = jnp.exp(jnp.sum(jnp.where(lane == h + H_DN, gt, 0.0), axis=-1, keepdims=True))
        S = s_ref[h]
        r = _dot(jnp.concatenate([k, q, pad], axis=0), S)
        v_new = beta * (v - eg * r[0:1])
        o = eg * r[1:2] + jnp.sum(q * k, axis=-1, keepdims=True) * v_new
        zeros7 = jnp.zeros((SUBLANES - 1, LANES), F32)
        outer = _dot_tn(jnp.concatenate([k, zeros7], axis=0), jnp.concatenate([v_new, zeros7], axis=0))
        sout_ref[h] = S * eg + outer
        o_ref[:, lanes] = (_rms(o, gain_ref[...]) * _silu(zg_ref[:, lanes])).astype(o_ref.dtype)


def _delta_step(z3, ba3, conv_state, conv_w, alog_row, dtb_row, gain_dn, state, layer, *, name):
    DB = z3.shape[0]
    W = H_DN * LANES

    def zspec(col0):
        return pl.BlockSpec((None, 1, W), lambda b: (b, 0, col0 // W))

    def cspec(col0):
        return pl.BlockSpec((None, None, CONV_W - 1, W), lambda b: (layer, b, 0, col0 // W))

    def wspec(col0):
        return pl.BlockSpec((CONV_W, W), lambda b: (0, col0 // W))

    row = pl.BlockSpec((1, LANES), lambda b: (0, 0))
    return pl.pallas_call(
        _delta_step_kernel,
        grid=(DB,),
        in_specs=[zspec(_C_QDN), zspec(_C_KDN), zspec(_C_VDN), zspec(_C_GATE),
                  pl.BlockSpec((None, 1, LANES), lambda b: (b, 0, 0)),
                  cspec(0), cspec(W), cspec(2 * W), wspec(0), wspec(W), wspec(2 * W), row, row, row,
                  pl.BlockSpec((None, None, H_DN, DN_DK, DN_DV), lambda b: (layer, b, 0, 0, 0))],
        out_specs=[pl.BlockSpec((None, 1, W), lambda b: (b, 0, 0)),
                   pl.BlockSpec((None, H_DN, DN_DK, DN_DV), lambda b: (b, 0, 0, 0))],
        out_shape=[jax.ShapeDtypeStruct((DB, 1, W), F32),
                   jax.ShapeDtypeStruct((DB, H_DN, DN_DK, DN_DV), F32)],
        compiler_params=pltpu.CompilerParams(dimension_semantics=("parallel",)),
        name=name,
    )(z3, z3, z3, z3, ba3, conv_state, conv_state, conv_state, conv_w, conv_w, conv_w,
      alog_row, dtb_row, gain_dn.reshape(1, LANES), state)


def _decode_kernel(pt_ref, qt_ref, knew_ref, vnew_ref, bias_ref, lam_ref, gain_ref, *rest, G, lam_init):
    m_s, l_s, acc_s = rest[2 * G + 1:]
    _decode_step(pl.program_id(1), pl.num_programs(1), qt_ref, knew_ref, vnew_ref, bias_ref, lam_ref, gain_ref,
                 rest[:G], rest[G:2 * G], rest[2 * G], m_s, l_s, acc_s, lam_init)


def _decode_step(step, nsteps, qt_ref, knew_ref, vnew_ref, bias_ref, lam_ref, gain_ref, k_refs, v_refs, o_ref,
                 m_s, l_s, acc_s, lam_init):
    G = len(k_refs)

    @pl.when(step == 0)
    def _():
        m_s[...] = jnp.full(m_s.shape, -jnp.inf, F32)
        l_s[...] = jnp.zeros_like(l_s)
        acc_s[...] = jnp.zeros_like(acc_s)

    qt = qt_ref[...]
    scores = []
    for i in range(G):
        s = lax.dot_general(qt, k_refs[i][...].astype(BF16), (((1,), (1,)), ((), ())),
                            preferred_element_type=F32)
        if i == G - 1:
            s = s + jnp.where(step == nsteps - 1, bias_ref[1], bias_ref[0])
        else:
            s = s + bias_ref[0]
        scores.append(s)
    m_old = m_s[...]
    m_new = m_old
    for s in scores:
        m_new = jnp.maximum(m_new, jnp.max(s, axis=-1, keepdims=True))
    alpha = jnp.exp(m_old - m_new)
    l_new = alpha * l_s[...]
    acc = alpha * acc_s[...]
    for i in range(G):
        p = jnp.exp(scores[i] - m_new)
        l_new = l_new + jnp.sum(p, axis=-1, keepdims=True)
        acc = acc + jnp.dot(p.astype(BF16), v_refs[i][...].astype(BF16), preferred_element_type=F32)
    m_s[...] = m_new
    l_s[...] = l_new
    acc_s[...] = acc

    @pl.when(step == nsteps - 1)
    def _():
        kn = knew_ref[...].astype(BF16).astype(F32)
        kn = jnp.concatenate([kn, kn], axis=0)
        vn = jnp.concatenate([vnew_ref[...], vnew_ref[...]], axis=0)
        s = jnp.sum(qt.astype(F32) * kn, axis=-1, keepdims=True) + bias_ref[2][:, 0:1]
        m_o = m_s[...]
        m_n = jnp.maximum(m_o, s)
        a = jnp.exp(m_o - m_n)
        p = jnp.exp(s - m_n)
        l_f = a * l_s[...] + p
        acc_f = (a * acc_s[...] + p * vn) / l_f
        o = acc_f[0:H_DIFF] - _lambda(lam_ref, lam_init) * acc_f[H_DIFF:2 * H_DIFF]
        o_ref[...] = _rms(o, gain_ref[...]) * (1.0 - lam_init)


def _decode_attn(qt, knew, vnew, dec_bias, lam_rows, gain_df, pages_k, pages_v, page_table, layer,
                 lam_init, *, G=16, name):
    DB, n_pages = page_table.shape
    PH = pages_k.shape[2]
    R = 2 * H_DIFF
    G = min(G, n_pages)
    assert n_pages % G == 0

    def pspec(i):
        return pl.BlockSpec((None, None, PH, LANES), lambda b, s, pt: (layer, pt[b, s * G + i], 0, 0))

    def per_b(rows):
        return pl.BlockSpec((None, rows, LANES), lambda b, s, pt: (b, 0, 0))

    row = pl.BlockSpec((1, LANES), lambda b, s, pt: (0, 0))
    blocks = [((PH, LANES), F32)] * (2 * G) + [((3, R, PH), F32)]
    grid_spec = pltpu.PrefetchScalarGridSpec(
        num_scalar_prefetch=1,
        grid=(DB, n_pages // G),
        in_specs=[per_b(R), per_b(H_DIFF), per_b(H_DIFF),
                  pl.BlockSpec((3, R, PH), lambda b, s, pt: (0, 0, 0)),
                  pl.BlockSpec((SUBLANES, LANES), lambda b, s, pt: (0, 0)), row]
        + [pspec(i) for i in range(G)] + [pspec(i) for i in range(G)],
        out_specs=per_b(H_DIFF),
        scratch_shapes=[pltpu.VMEM((R, 1), F32), pltpu.VMEM((R, 1), F32), pltpu.VMEM((R, LANES), F32)],
    )
    return pl.pallas_call(
        functools.partial(_decode_kernel, G=G, lam_init=lam_init),
        grid_spec=grid_spec,
        out_shape=jax.ShapeDtypeStruct((DB, H_DIFF, LANES), F32),
        compiler_params=pltpu.CompilerParams(
            dimension_semantics=("parallel", "arbitrary"), vmem_limit_bytes=_vmem_limit(blocks)),
        name=name,
    )(page_table, qt, knew, vnew, dec_bias, lam_rows, gain_df.reshape(1, LANES),
      *([pages_k] * G), *([pages_v] * G))


def _mlp_decode_kernel(pt_ref, x_ref, g_ref, wu_ref, wd_ref, *rest, G, n_groups, lam_init, has_final):
    del pt_ref
    rest = list(rest)
    gf_ref = rest.pop(0) if has_final else None
    qt_ref, knew_ref, vnew_ref, bias_ref, lam_ref, gain_ref = rest[:6]
    k_refs, v_refs = rest[6:6 + G], rest[6 + G:6 + 2 * G]
    o_ref, od_ref, h_ref, acc_ref, m_s, l_s, a_s = rest[6 + 2 * G:]
    _mlp_step(x_ref, g_ref, wu_ref, wd_ref, gf_ref, o_ref, h_ref, acc_ref)
    flat = pl.program_id(0) * pl.num_programs(1) + pl.program_id(1)
    _decode_step(flat % n_groups, n_groups, qt_ref, knew_ref, vnew_ref, bias_ref, lam_ref, gain_ref,
                 k_refs, v_refs, od_ref, m_s, l_s, a_s, lam_init)


def _fused_pages_per_step(M, FF, DB, n_pages, tm=512, tf=1024):
    steps = (M // min(tm, M)) * (FF // min(tf, FF))
    total = DB * n_pages
    if total % steps or n_pages % (total // steps):
        return None
    return total // steps


def _mlp_decode(x, gain, w_up, w_down, qt, knew, vnew, dec_bias, lam_rows, gain_df, pages_k, pages_v, page_table,
                layer, lam_init, *, final_gain=None, tm=512, tf=1024, name):
    M, D = x.shape
    FF = w_up.shape[1]
    tm, tf = min(tm, M), min(tf, FF)
    nf = FF // tf
    DB, n_pages = page_table.shape
    G = _fused_pages_per_step(M, FF, DB, n_pages, tm, tf)
    assert G is not None and M % tm == 0 and FF % tf == 0
    n_groups = n_pages // G
    PH = pages_k.shape[2]
    R = 2 * H_DIFF

    def pspec(k):
        return pl.BlockSpec((None, None, PH, LANES),
                            lambda i, f, pt: (layer, pt[(i * nf + f) // n_groups, ((i * nf + f) % n_groups) * G + k], 0, 0))

    def per_b(rows):
        return pl.BlockSpec((None, rows, LANES), lambda i, f, pt: ((i * nf + f) // n_groups, 0, 0))

    in_specs = [pl.BlockSpec((tm, D), lambda i, f, pt: (i, 0)),
                pl.BlockSpec((1, D), lambda i, f, pt: (0, 0)),
                pl.BlockSpec((D, tf), lambda i, f, pt: (0, f)),
                pl.BlockSpec((tf, D), lambda i, f, pt: (f, 0))]
    args = [x, gain.reshape(1, D), w_up, w_down]
    if final_gain is not None:
        in_specs.append(pl.BlockSpec((1, D), lambda i, f, pt: (0, 0)))
        args.append(final_gain.reshape(1, D))
    in_specs += [per_b(R), per_b(H_DIFF), per_b(H_DIFF),
                 pl.BlockSpec((3, R, PH), lambda i, f, pt: (0, 0, 0)),
                 pl.BlockSpec((SUBLANES, LANES), lambda i, f, pt: (0, 0)),
                 pl.BlockSpec((1, LANES), lambda i, f, pt: (0, 0))]
    in_specs += [pspec(k) for k in range(G)] + [pspec(k) for k in range(G)]
    args += [qt, knew, vnew, dec_bias, lam_rows, gain_df.reshape(1, LANES)] + [pages_k] * G + [pages_v] * G
    blocks = ([((tm, D), F32), ((D, tf), BF16), ((tf, D), BF16), ((tm, D), F32)]
              + [((PH, LANES), F32)] * (2 * G) + [((3, R, PH), F32)])
    resident = [((tm, D), BF16), ((tm, D), F32)]
    grid_spec = pltpu.PrefetchScalarGridSpec(
        num_scalar_prefetch=1,
        grid=(M // tm, nf),
        in_specs=in_specs,
        out_specs=[pl.BlockSpec((tm, D), lambda i, f, pt: (i, 0)), per_b(H_DIFF)],
        scratch_shapes=[pltpu.VMEM((tm, D), BF16), pltpu.VMEM((tm, D), F32),
                        pltpu.VMEM((R, 1), F32), pltpu.VMEM((R, 1), F32), pltpu.VMEM((R, LANES), F32)],
    )
    return pl.pallas_call(
        functools.partial(_mlp_decode_kernel, G=G, n_groups=n_groups, lam_init=lam_init,
                          has_final=final_gain is not None),
        grid_spec=grid_spec,
        out_shape=[jax.ShapeDtypeStruct((M, D), F32), jax.ShapeDtypeStruct((DB, H_DIFF, LANES), F32)],
        compiler_params=pltpu.CompilerParams(
            dimension_semantics=("arbitrary", "arbitrary"), vmem_limit_bytes=_vmem_limit(blocks, resident)),
        name=name,
    )(page_table, *args)


def _pad_rows(x, rows):
    return jnp.pad(x, ((0, rows - x.shape[0]), (0, 0)))


def kernel(x_prompt, x_sample, cache_k, cache_v, state_delta, state_conv, page_table, rel_bias, norm_mix, w_in, conv_w, a_log, dt_bias, norm_dn, w_o_dn, lam_q1, lam_k1, lam_q2, lam_k2, norm_diff, w_o_diff, w_out, norm_mlp, w_up, w_down, norm_final):
    B, T, D = x_prompt.shape
    DB = x_sample.shape[0]
    depth = w_in.shape[0]
    n_pool, page = cache_k.shape[1], cache_k.shape[2]
    conv_dim = conv_w.shape[-1]
    n_ba = 2 * H_DN
    assert conv_dim == 3 * H_DN * DN_DK and x_sample.shape[1] == 1
    c_gate = conv_dim + n_ba
    c_kdf = c_gate + H_DN * DN_DV + H_DIFF * 2 * DIFF_DQ
    c_gbr = c_kdf + H_DIFF * 2 * DIFF_DQ + H_DIFF * DIFF_DV
    f_widths = (H_DIFF * 2 * DIFF_DQ, H_DIFF * DIFF_DV, LANES)
    MS = 16
    tq = min(512, T)
    n_tail = B * (CONV_W - 1)
    assert n_tail <= MS

    xp = x_prompt.reshape(B * T, D)
    xs = _pad_rows(x_sample.reshape(DB, D), MS)
    diag, sub = _bias_tiles(rel_bias, tq, tq)
    fuse_mlp_decode = _fused_pages_per_step(B * T, w_up.shape[-1], DB, page_table.shape[1]) is not None
    dec_bias = _dec_bias(rel_bias, page)
    pages_k = cache_k.reshape(depth, n_pool, page * H_DIFF, 2 * DIFF_DQ)
    pages_v = cache_v.reshape(depth, n_pool, page * H_DIFF, DIFF_DV)
    in_map0 = jnp.arange(2 * DIFF_DQ) < DIFF_DQ

    kp_l, vp_l, sp_l, cp_l, ks_l, vs_l, ss_l, cs_l = [], [], [], [], [], [], [], []
    for l in range(depth):
        lam_init = 0.8 - 0.6 * math.exp(-0.3 * l)
        wl = w_in[l]
        w_b = jnp.concatenate([wl[:, :conv_dim], wl[:, c_gate:c_kdf], wl[:, c_gbr:]], axis=1).astype(BF16)
        w_f = jnp.pad(jnp.concatenate([wl[:, c_kdf:c_gbr], wl[:, conv_dim:c_gate]], axis=1),
                      ((0, 0), (0, LANES - n_ba))).astype(BF16)
        w_odn, w_odf, w_o = w_o_dn[l].astype(BF16), w_o_diff[l].astype(BF16), w_out[l].astype(BF16)
        alog_row = jnp.pad(a_log[l], (H_DN, LANES - n_ba)).reshape(1, LANES)
        dtb_row = jnp.pad(dt_bias[l], (H_DN, LANES - n_ba)).reshape(1, LANES)
        lam_rows = jnp.pad(jnp.stack([lam_q1[l], lam_k1[l], lam_q2[l], lam_k2[l]]),
                           ((0, SUBLANES - 4), (0, LANES - DIFF_DQ)))
        last = l == depth - 1

        x_tail = _pad_rows(xp.reshape(B, T, D)[:, T - (CONV_W - 1):].reshape(n_tail, D), MS)
        zst = _mm(jnp.concatenate([xs, x_tail], axis=0), w_b, gain=norm_mix[l], tn=1024, name=f"in_proj_s{l}")
        zs, z_tail = zst[:MS], zst[MS:]
        zb = _mm(xp, w_b, gain=norm_mix[l], out_dtype=BF16, tn=1536, name=f"in_proj_p{l}")
        kf, vf, ba = _mm_split(xp, w_f, norm_mix[l], f_widths, name=f"in_proj_kv_p{l}")
        o_dn, s_fin, w_u, w_d = _delta_prompt(zb, ba, conv_w[l], alog_row, dtb_row, norm_dn[l], B, T,
                                              (w_up, w_down, l), name=f"delta_p{l}")
        o_df = _attn_prompt(zb, kf, vf, diag, sub, lam_rows, norm_diff[l], B, T, lam_init, tq=tq,
                            name=f"attn_p{l}")
        merged = _merge(o_dn, o_df, zb, w_odn, w_odf, name=f"merge_p{l}")
        xp = _mm(merged, w_o, residual=xp, tn=1024, name=f"out_proj_p{l}")
        kp_l.append(kf.reshape(B, T, H_DIFF, 2 * DIFF_DQ))
        vp_l.append(vf.reshape(B, T, H_DIFF, DIFF_DV))
        sp_l.append(s_fin)
        cp_l.append(z_tail[:n_tail, :conv_dim].reshape(B, CONV_W - 1, conv_dim))

        ks16, vs16, bas16 = _mm_split(xs, w_f, norm_mix[l], f_widths, name=f"in_proj_kv_s{l}")
        zs3 = zs[:DB].reshape(DB, 1, -1)
        o_dn_s, s_new = _delta_step(zs3, bas16[:DB].reshape(DB, 1, LANES), state_conv,
                                    conv_w[l], alog_row, dtb_row, norm_dn[l], state_delta, l, name=f"delta_s{l}")
        q_s = zs[:DB, _C_QDF:_C_QDF + 1024].reshape(DB, H_DIFF, 2 * DIFF_DQ) * (DIFF_DQ ** -0.5)
        qt = jnp.concatenate([jnp.where(in_map0, q_s, 0.0), jnp.where(in_map0, 0.0, q_s)], axis=1).astype(BF16)
        k_s = ks16[:DB].reshape(DB, 1, -1)
        v_s = vs16[:DB].reshape(DB, 1, -1)
        dec_args = (qt, k_s.reshape(DB, H_DIFF, 2 * DIFF_DQ), v_s.reshape(DB, H_DIFF, DIFF_DV),
                    dec_bias, lam_rows, norm_diff[l], pages_k, pages_v, page_table, l, lam_init)
        final_gain = norm_final if last else None
        if fuse_mlp_decode:
            xp, o_df_s = _mlp_decode(xp, norm_mlp[l], w_u, w_d, *dec_args, final_gain=final_gain,
                                     name=f"mlp_p_attn_s{l}")
        else:
            xp = _mlp(xp, norm_mlp[l], w_u, w_d, final_gain=final_gain, name=f"mlp_p{l}")
            o_df_s = _decode_attn(*dec_args, name=f"attn_s{l}")
        merged_s = _merge(_pad_rows(o_dn_s.reshape(DB, -1), MS).astype(BF16),
                          _pad_rows(o_df_s.reshape(DB, -1), MS).astype(BF16), zs,
                          w_odn, w_odf, tn=1024, name=f"merge_s{l}")
        xs = _mm(merged_s, w_o, residual=xs, tn=1024, name=f"out_proj_s{l}")
        xs = _mlp(xs, norm_mlp[l], w_u, w_d, final_gain=norm_final if last else None, name=f"mlp_s{l}")
        ks_l.append(k_s.reshape(DB, 1, H_DIFF, 2 * DIFF_DQ))
        vs_l.append(v_s.reshape(DB, 1, H_DIFF, DIFF_DV))
        ss_l.append(s_new)
        cs_l.append(jnp.concatenate([state_conv[l][:, 1:], zs3[:, :, :conv_dim]], axis=1))

    y_prompt = xp.reshape(B, T, D)
    y_sample = xs[:DB].reshape(DB, 1, D)
    return (y_prompt, y_sample,
            jnp.stack(kp_l), jnp.stack(vp_l), jnp.stack(sp_l), jnp.stack(cp_l),
            jnp.stack(ks_l), jnp.stack(vs_l), jnp.stack(ss_l), jnp.stack(cs_l))
```
